```python
import jax, jax.numpy as jnp
from jax import lax
import numpy as np

D_MODEL = 4096
BATCH = 2
SEQ = 8192
DEPTH = 1

N_META = 16
BLOCK = 128
WINDOW = 128
ATT_HEADS = 32
ATT_KV_HEADS = 4
ATT_HEAD_DIM = 64
ATT_GROUP = ATT_HEADS // ATT_KV_HEADS
ATT_WIDTH = ATT_HEADS * ATT_HEAD_DIM
ATT_KV_WIDTH = ATT_KV_HEADS * ATT_HEAD_DIM
RET_HEADS = 8
RET_QK_DIM = 256
RET_V_DIM = 256
RET_QK_WIDTH = RET_HEADS * RET_QK_DIM
RET_WIDTH = RET_HEADS * RET_V_DIM
MIX_WIDTH = ATT_WIDTH + RET_WIDTH
IN_WIDTH = ATT_WIDTH + 2 * ATT_KV_WIDTH + 2 * RET_QK_WIDTH + 2 * RET_WIDTH
SPLIT_OFFSETS = (
    ATT_WIDTH,
    ATT_WIDTH + ATT_KV_WIDTH,
    ATT_WIDTH + 2 * ATT_KV_WIDTH,
    ATT_WIDTH + 2 * ATT_KV_WIDTH + RET_QK_WIDTH,
    ATT_WIDTH + 2 * ATT_KV_WIDTH + 2 * RET_QK_WIDTH,
    ATT_WIDTH + 2 * ATT_KV_WIDTH + 2 * RET_QK_WIDTH + RET_WIDTH,
)
D_FF = 11008
FFN_HALF = 0.5
NORM_EPS = 1e-6
MASK_VALUE = -1e30

kernel_name = "hybrid_swa_retention_macaron_layer"


def rmsnorm(x, gain):
    xf = x.astype(jnp.float32)
    y = xf * lax.rsqrt(jnp.mean(xf * xf, axis=-1, keepdims=True) + NORM_EPS)
    return (y * gain.astype(jnp.float32)).astype(x.dtype)


def swiglu_ffn(x, w_gate_up, w_down):
    gate, up = jnp.split(x @ w_gate_up, 2, axis=-1)
    return (jax.nn.silu(gate) * up) @ w_down


def alibi_slopes():
    s = 2.0 ** (-8.0 * np.arange(1, ATT_HEADS + 1) / ATT_HEADS)
    return jnp.asarray(s, dtype=jnp.float32).reshape(ATT_KV_HEADS, ATT_GROUP)


def sliding_window_gqa(q, k, v, sinks, n_pad):
    B, Lp, _ = q.shape
    nb = Lp // BLOCK
    qb = q.reshape(B, nb, BLOCK, ATT_KV_HEADS, ATT_GROUP, ATT_HEAD_DIM)
    kb = k.reshape(B, nb, BLOCK, ATT_KV_HEADS, ATT_HEAD_DIM)
    vb = v.reshape(B, nb, BLOCK, ATT_KV_HEADS, ATT_HEAD_DIM)
    shift = ((0, 0), (1, 0), (0, 0), (0, 0), (0, 0))
    kk = jnp.concatenate([jnp.pad(kb[:, :-1], shift), kb], axis=2)
    vv = jnp.concatenate([jnp.pad(vb[:, :-1], shift), vb], axis=2)

    dist = BLOCK + jnp.arange(BLOCK)[:, None] - jnp.arange(2 * BLOCK)[None, :]
    band = (dist >= 0) & (dist < WINDOW)
    k_abs = (jnp.arange(nb)[:, None] - 1) * BLOCK + jnp.arange(2 * BLOCK)[None, :]
    valid = band[None] & (k_abs >= n_pad)[:, None, :]
    alibi = -alibi_slopes()[:, :, None, None] * dist.astype(jnp.float32)

    scores = jnp.einsum('bnqhgd,bnkhd->bnhgqk', qb, kk).astype(jnp.float32) * (ATT_HEAD_DIM ** -0.5)
    scores = jnp.where(valid[:, None, None], scores + alibi, MASK_VALUE)

    sink = sinks.astype(jnp.float32).reshape(ATT_KV_HEADS, ATT_GROUP)[None, None, :, :, None]
    m = jnp.maximum(scores.max(axis=-1), sink)
    p = jnp.exp(scores - m[..., None])
    denom = p.sum(axis=-1) + jnp.exp(sink - m)
    out = jnp.einsum('bnhgqk,bnkhd->bnqhgd', p.astype(v.dtype), vv)
    out = out / denom.transpose(0, 1, 4, 2, 3)[..., None].astype(out.dtype)
    return out.reshape(B, Lp, ATT_WIDTH)


def chunkwise_retention(q, k, v):
    B, Lp, H, dk = q.shape
    dv = v.shape[-1]
    nb = Lp // BLOCK
    log_gamma = jnp.log(1.0 - 2.0 ** (-5.0 - jnp.arange(H, dtype=jnp.float32)))
    pos = jnp.arange(BLOCK, dtype=jnp.float32)
    qc = q.reshape(B, nb, BLOCK, H, dk)
    kc = k.reshape(B, nb, BLOCK, H, dk) * (dk ** -0.5)
    vc = v.reshape(B, nb, BLOCK, H, dv)

    rel = pos[:, None] - pos[None, :]
    decay = jnp.where(rel >= 0, jnp.exp(jnp.maximum(rel, 0.0)[None] * log_gamma[:, None, None]), 0.0)
    s = jnp.einsum('bnihd,bnjhd->bnhij', qc, kc) * decay.astype(q.dtype)
    inner = jnp.einsum('bnhij,bnjhe->bnihe', s, vc)

    k_w = jnp.exp((BLOCK - 1 - pos)[:, None] * log_gamma[None, :]).astype(k.dtype)
    kv = jnp.einsum('bnjhd,bnjhe->nbhde', kc * k_w[:, :, None], vc)
    chunk_decay = jnp.exp(BLOCK * log_gamma).astype(kv.dtype)[:, None, None]

    def step(state, kv_n):
        return state * chunk_decay + kv_n, state

    _, prev_states = lax.scan(step, jnp.zeros((B, H, dk, dv), kv.dtype), kv)
    q_w = jnp.exp((pos + 1.0)[:, None] * log_gamma[None, :]).astype(q.dtype)
    cross = jnp.einsum('bnihd,nbhde->bnihe', qc * q_w[:, :, None], prev_states)
    return (inner + cross).reshape(B, Lp, H, dv)


def hybrid_mixer(h, w_in, b_in, attn_sinks, w_out):
    B, L, _ = h.shape
    n_pad = BLOCK - N_META
    proj = jnp.pad(h @ w_in + b_in, ((0, 0), (n_pad, 0), (0, 0)))
    Lp = proj.shape[1]
    q_a, k_a, v_a, q_r, k_r, v_r, g_r = jnp.split(proj, SPLIT_OFFSETS, axis=-1)

    attn = sliding_window_gqa(q_a, k_a, v_a, attn_sinks, n_pad)

    ret = chunkwise_retention(q_r.reshape(B, Lp, RET_HEADS, RET_QK_DIM),
                              k_r.reshape(B, Lp, RET_HEADS, RET_QK_DIM),
                              v_r.reshape(B, Lp, RET_HEADS, RET_V_DIM))
    rf = ret.astype(jnp.float32)
    rf = rf * lax.rsqrt(jnp.mean(rf * rf, axis=-1, keepdims=True) + NORM_EPS)
    ret = rf.astype(h.dtype).reshape(B, Lp, RET_WIDTH) * jax.nn.silu(g_r)

    merged = jnp.concatenate([attn, ret], axis=-1)[:, n_pad:]
    return merged @ w_out


def setup_inputs(seed: int = 0) -> dict:
    key = jax.random.key(seed)
    ks = jax.random.split(key, 16)
    f32 = jnp.float32

    def normal(k, shape, scale):
        return jax.random.normal(k, shape, f32) * scale

    def gain(k):
        return 1.0 + normal(k, (DEPTH, D_MODEL), 0.02)

    return {
        "x": normal(ks[0], (BATCH, SEQ, D_MODEL), 1.0),
        "meta_tokens": normal(ks[1], (N_META, D_MODEL), 1.0),
        "norm_ffn1": gain(ks[2]),
        "w_ffn1_gate_up": normal(ks[3], (DEPTH, D_MODEL, 2 * D_FF), D_MODEL ** -0.5),
        "w_ffn1_down": normal(ks[4], (DEPTH, D_FF, D_MODEL), D_FF ** -0.5),
        "norm_mix": gain(ks[5]),
        "w_in": normal(ks[6], (DEPTH, D_MODEL, IN_WIDTH), D_MODEL ** -0.5),
        "b_in": normal(ks[7], (DEPTH, IN_WIDTH), 0.02),
        "attn_sinks": normal(ks[8], (DEPTH, ATT_HEADS), 0.5),
        "w_out": normal(ks[9], (DEPTH, MIX_WIDTH, D_MODEL), MIX_WIDTH ** -0.5),
        "norm_ffn2": gain(ks[10]),
        "w_ffn2_gate_up": normal(ks[11], (DEPTH, D_MODEL, 2 * D_FF), D_MODEL ** -0.5),
        "w_ffn2_down": normal(ks[12], (DEPTH, D_FF, D_MODEL), D_FF ** -0.5),
        "norm_final": 1.0 + normal(ks[13], (D_MODEL,), 0.02),
    }


def reference(x, meta_tokens, norm_ffn1, w_ffn1_gate_up, w_ffn1_down, norm_mix, w_in, b_in,
              attn_sinks, w_out, norm_ffn2, w_ffn2_gate_up, w_ffn2_down, norm_final):
    B = x.shape[0]
    meta = jnp.broadcast_to(meta_tokens[None].astype(x.dtype), (B, N_META, D_MODEL))
    h = jnp.concatenate([meta, x], axis=1)
    for layer in range(DEPTH):
        h = h + FFN_HALF * swiglu_ffn(rmsnorm(h, norm_ffn1[layer]), w_ffn1_gate_up[layer], w_ffn1_down[layer])
        h = h + hybrid_mixer(rmsnorm(h, norm_mix[layer]), w_in[layer], b_in[layer], attn_sinks[layer], w_out[layer])
        h = h + FFN_HALF * swiglu_ffn(rmsnorm(h, norm_ffn2[layer]), w_ffn2_gate_up[layer], w_ffn2_down[layer])
    h = rmsnorm(h, norm_final)
    return h[:, N_META:]
```

```python
import functools

import numpy as np
import jax
import jax.numpy as jnp
from jax import lax
from jax.experimental import pallas as pl
from jax.experimental.pallas import tpu as pltpu

F32 = jnp.float32
BF16 = jnp.bfloat16

N_META = 16
BLOCK = 128
ATT_HEADS = 32
ATT_KV_HEADS = 4
ATT_HEAD_DIM = 64
ATT_GROUP = ATT_HEADS // ATT_KV_HEADS
ATT_WIDTH = ATT_HEADS * ATT_HEAD_DIM
ATT_KV_WIDTH = ATT_KV_HEADS * ATT_HEAD_DIM
RET_HEADS = 8
RET_DIM = 256
RET_WIDTH = RET_HEADS * RET_DIM
MIX_WIDTH = ATT_WIDTH + RET_WIDTH
IN_WIDTH = ATT_WIDTH + 2 * ATT_KV_WIDTH + 4 * RET_WIDTH
FFN_HALF = 0.5
NORM_EPS = 1e-6
MASK_VALUE = -1e30

LANES = 128
PROJ_BN = 512
VMEM_LIMIT_BYTES = 56 * 1024 * 1024


def _params(*semantics):
    return pltpu.CompilerParams(dimension_semantics=semantics, vmem_limit_bytes=VMEM_LIMIT_BYTES)


def _block(total, preferred):
    if total <= preferred:
        return total
    b = preferred
    while total % b:
        b //= 2
    return b


def _rmsnorm_kernel(x_ref, g_ref, o_ref):
    x = x_ref[...]
    y = x * lax.rsqrt(jnp.mean(x * x, axis=-1, keepdims=True) + NORM_EPS)
    o_ref[...] = (y * g_ref[...]).astype(o_ref.dtype)


def _rmsnorm(x, gain, out_dtype):
    m, d = x.shape
    bm = _block(m, 256)
    return pl.pallas_call(
        _rmsnorm_kernel,
        grid=(m // bm,),
        in_specs=[pl.BlockSpec((bm, d), lambda i: (i, 0)),
                  pl.BlockSpec((1, d), lambda i: (0, 0))],
        out_specs=pl.BlockSpec((bm, d), lambda i: (i, 0)),
        out_shape=jax.ShapeDtypeStruct((m, d), out_dtype),
        compiler_params=_params("parallel"),
        name="rmsnorm",
    )(x, gain.reshape(1, d).astype(F32))


def _swiglu_kernel(x_ref, wg_ref, wu_ref, o_ref):
    x = x_ref[...]
    g = jnp.dot(x, wg_ref[...], preferred_element_type=F32)
    u = jnp.dot(x, wu_ref[...], preferred_element_type=F32)
    o_ref[...] = (g * jax.nn.sigmoid(g) * u).astype(o_ref.dtype)


def _gate_up_swiglu(xn, w_gate_up):
    m, d = xn.shape
    f = w_gate_up.shape[1] // 2
    bm = _block(m, 1024)
    bf = _block(f, 256)
    nf = f // bf
    return pl.pallas_call(
        _swiglu_kernel,
        grid=(m // bm, nf),
        in_specs=[pl.BlockSpec((bm, d), lambda i, j: (i, 0)),
                  pl.BlockSpec((d, bf), lambda i, j: (0, j)),
                  pl.BlockSpec((d, bf), lambda i, j: (0, j + nf))],
        out_specs=pl.BlockSpec((bm, bf), lambda i, j: (i, j)),
        out_shape=jax.ShapeDtypeStruct((m, f), BF16),
        compiler_params=_params("parallel", "arbitrary"),
        name="gate_up_swiglu",
    )(xn, w_gate_up, w_gate_up)


def _matmul_residual_kernel(a_ref, w_ref, r_ref, o_ref, *, scale, nk):
    part = jnp.dot(a_ref[...], w_ref[...], preferred_element_type=F32)
    if nk == 1:
        o_ref[...] = r_ref[...] + scale * part
        return
    k = pl.program_id(2)

    @pl.when(k == 0)
    def _first():
        o_ref[...] = part

    @pl.when(jnp.logical_and(k > 0, k < nk - 1))
    def _middle():
        o_ref[...] += part

    @pl.when(k == nk - 1)
    def _last():
        o_ref[...] = r_ref[...] + scale * (o_ref[...] + part)


def _matmul_residual(a, w, resid, scale, bk_pref):
    m, kdim = a.shape
    n = w.shape[1]
    bm = _block(m, 1024)
    bn = _block(n, 512)
    bk = bk_pref if (bk_pref is not None and kdim % bk_pref == 0) else kdim
    nk = kdim // bk
    return pl.pallas_call(
        functools.partial(_matmul_residual_kernel, scale=scale, nk=nk),
        grid=(m // bm, n // bn, nk),
        in_specs=[pl.BlockSpec((bm, bk), lambda i, j, k: (i, k)),
                  pl.BlockSpec((bk, bn), lambda i, j, k: (k, j)),
                  pl.BlockSpec((bm, bn), lambda i, j, k: (i, j))],
        out_specs=pl.BlockSpec((bm, bn), lambda i, j, k: (i, j)),
        out_shape=jax.ShapeDtypeStruct((m, n), F32),
        compiler_params=_params("parallel", "parallel", "arbitrary"),
        name="matmul_residual",
    )(a, w, resid)


_QKV_BLOCKS = ATT_WIDTH // PROJ_BN
_N_PROJ_BLOCKS = IN_WIDTH // PROJ_BN
SEG_QA, SEG_QR, SEG_KR, SEG_VR, SEG_GR = 0, 1, 2, 3, 4
SEG_KVA = _N_PROJ_BLOCKS - 1


def _proj_out_block(j):
    return jnp.where(j < _QKV_BLOCKS, j, jnp.where(j == _QKV_BLOCKS, _N_PROJ_BLOCKS - 1, j - 1))


def _in_proj_kernel(x_ref, w_ref, b_ref, o_ref):
    acc = jnp.dot(x_ref[...], w_ref[...], preferred_element_type=F32)
    o_ref[...] = (acc + b_ref[...]).astype(o_ref.dtype)


def _in_proj(xn, w_in, b_in):
    m, d = xn.shape
    bm = _block(m, 1024)
    return pl.pallas_call(
        _in_proj_kernel,
        grid=(m // bm, _N_PROJ_BLOCKS),
        in_specs=[pl.BlockSpec((bm, d), lambda i, j: (i, 0)),
                  pl.BlockSpec((d, PROJ_BN), lambda i, j: (0, j)),
                  pl.BlockSpec((1, PROJ_BN), lambda i, j: (0, j))],
        out_specs=pl.BlockSpec((bm, PROJ_BN), lambda i, j: (i, _proj_out_block(j))),
        out_shape=jax.ShapeDtypeStruct((m, IN_WIDTH), BF16),
        compiler_params=_params("parallel", "arbitrary"),
        name="in_proj",
    )(xn, w_in, b_in.reshape(1, IN_WIDTH).astype(F32))


def _retention_constants():
    h = np.arange(RET_HEADS, dtype=np.float64)
    log_gamma = np.log(1.0 - 2.0 ** (-5.0 - h))
    pos = np.arange(BLOCK, dtype=np.float64)
    rel = pos[:, None] - pos[None, :]
    decay = np.where(rel >= 0, np.exp(np.maximum(rel, 0.0)[None] * log_gamma[:, None, None]), 0.0)
    k_w = np.exp((BLOCK - 1 - pos)[None, :] * log_gamma[:, None])
    q_w = np.exp((pos + 1.0)[None, :] * log_gamma[:, None])
    chunk_decay = np.exp(BLOCK * log_gamma)
    wide = lambda w: np.broadcast_to(w[:, :, None], (RET_HEADS, BLOCK, RET_DIM))
    return (jnp.asarray(decay, F32), jnp.asarray(wide(k_w), F32), jnp.asarray(wide(q_w), F32),
            [float(c) for c in chunk_decay])


def _alibi_slopes():
    return [float(2.0 ** (-8.0 * i / ATT_HEADS)) for i in range(1, ATT_HEADS + 1)]


def _lane_halves(x):
    lane = lax.broadcasted_iota(jnp.int32, x.shape, 1)
    swapped = pltpu.roll(x, ATT_HEAD_DIM, 1)
    low = lane < ATT_HEAD_DIM
    return jnp.where(low, x, swapped), jnp.where(low, swapped, x)


def _retention_state_update(k_scaled, v, kw, state_prev, chunk_decay):
    kd = (k_scaled * kw).astype(BF16)
    kv = lax.dot_general(kd, v, (((0,), (0,)), ((), ())), preferred_element_type=F32)
    if state_prev is None:
        return kv
    return state_prev * chunk_decay + kv


def _mixer_kernel(sink_ref, qa_ref, qr_ref, kr_ref, vr_ref, gr_ref, kva_ref, kva_prev_ref,
                  mkr_ref, mvr_ref, mkva_ref, decay_ref, kw_ref, qw_ref,
                  o_ref, state_ref, *, chunk_decay, slopes):
    c = pl.program_id(1)
    k_scale = RET_DIM ** -0.5

    @pl.when(c == 0)
    def _from_meta_chunk():
        for h in range(RET_HEADS):
            cols = slice(h * RET_DIM, (h + 1) * RET_DIM)
            km = mkr_ref[:, cols].astype(F32) * k_scale
            state_ref[h] = _retention_state_update(km, mvr_ref[:, cols], kw_ref[h], None, None)

    kv_prev = jnp.where(c == 0, mkva_ref[...], kva_prev_ref[...])
    kv_cat = jnp.concatenate([kv_prev, kva_ref[...]], axis=0).astype(F32)
    row = lax.broadcasted_iota(jnp.int32, (BLOCK, 2 * BLOCK), 0)
    col = lax.broadcasted_iota(jnp.int32, (BLOCK, 2 * BLOCK), 1)
    dist = BLOCK + row - col
    first_key = jnp.where(c == 0, BLOCK - N_META, 0)
    valid = (dist >= 0) & (dist < BLOCK) & (col >= first_key)
    distf = dist.astype(F32)
    lane = lax.broadcasted_iota(jnp.int32, (BLOCK, LANES), 1)
    low = lane < ATT_HEAD_DIM
    att_scale = ATT_HEAD_DIM ** -0.5

    kdup, vdup = [], []
    for t in range(ATT_KV_WIDTH // LANES):
        k_lo, k_hi = _lane_halves(kv_cat[:, t * LANES:(t + 1) * LANES])
        v_lo, v_hi = _lane_halves(kv_cat[:, ATT_KV_WIDTH + t * LANES:ATT_KV_WIDTH + (t + 1) * LANES])
        kdup += [k_lo.astype(BF16), k_hi.astype(BF16)]
        vdup += [v_lo.astype(BF16), v_hi.astype(BF16)]

    zero = jnp.zeros((BLOCK, LANES), BF16)
    for h in range(ATT_KV_HEADS):
        q_rows = []
        for p in range(ATT_GROUP // 2):
            q2 = qa_ref[:, (h * ATT_GROUP + 2 * p) * ATT_HEAD_DIM:(h * ATT_GROUP + 2 * p + 2) * ATT_HEAD_DIM]
            q_rows += [jnp.where(low, q2, zero), jnp.where(low, zero, q2)]
        q_all = jnp.concatenate(q_rows, axis=0)
        s_all = lax.dot_general(q_all, kdup[h], (((1,), (1,)), ((), ())), preferred_element_type=F32)
        probs, denoms = [], []
        for g in range(ATT_GROUP):
            head = h * ATT_GROUP + g
            s = s_all[g * BLOCK:(g + 1) * BLOCK] * att_scale
            s = jnp.where(valid, s - slopes[head] * distf, MASK_VALUE)
            sink = sink_ref[head]
            m = jnp.maximum(jnp.max(s, axis=-1, keepdims=True), sink)
            p_ = jnp.exp(s - m)
            denoms.append(jnp.sum(p_, axis=-1, keepdims=True) + jnp.exp(sink - m))
            probs.append(p_.astype(BF16))
        pv = jnp.dot(jnp.concatenate(probs, axis=0), vdup[h], preferred_element_type=F32)
        for p in range(ATT_GROUP // 2):
            even = pv[(2 * p) * BLOCK:(2 * p + 1) * BLOCK] / denoms[2 * p]
            odd = pv[(2 * p + 1) * BLOCK:(2 * p + 2) * BLOCK] / denoms[2 * p + 1]
            c0 = (h * ATT_GROUP + 2 * p) * ATT_HEAD_DIM
            o_ref[:, c0:c0 + LANES] = jnp.where(low, even, odd).astype(o_ref.dtype)

    for h in range(RET_HEADS):
        cols = slice(h * RET_DIM, (h + 1) * RET_DIM)
        q = qr_ref[:, cols]
        v = vr_ref[:, cols]
        k_scaled = kr_ref[:, cols].astype(F32) * k_scale
        s = lax.dot_general(q, k_scaled.astype(BF16), (((1,), (1,)), ((), ())), preferred_element_type=F32)
        s = s * decay_ref[h]
        inner = jnp.dot(s.astype(BF16), v, preferred_element_type=F32)
        state = state_ref[h]
        qd = (q.astype(F32) * qw_ref[h]).astype(BF16)
        cross = jnp.dot(qd, state.astype(BF16), preferred_element_type=F32)
        state_ref[h] = _retention_state_update(k_scaled, v, kw_ref[h], state, chunk_decay[h])
        ret = inner + cross
        ret = ret * lax.rsqrt(jnp.mean(ret * ret, axis=-1, keepdims=True) + NORM_EPS)
        gate = gr_ref[:, cols].astype(F32)
        o_ref[:, ATT_WIDTH + h * RET_DIM:ATT_WIDTH + (h + 1) * RET_DIM] = (
            ret * (gate * jax.nn.sigmoid(gate))).astype(o_ref.dtype)


def _mixer_core(proj, proj_meta, sinks, batch, seq):
    nchunks = seq // BLOCK
    decay, k_w, q_w, chunk_decay = _retention_constants()
    proj3 = proj.reshape(batch, seq, IN_WIDTH)
    meta_chunk = jnp.pad(proj_meta, ((BLOCK - N_META, 0), (0, 0)))
    seg = lambda s: pl.BlockSpec((None, BLOCK, ATT_WIDTH), lambda b, c: (b, c, s))
    meta_seg = lambda s: pl.BlockSpec((BLOCK, ATT_WIDTH), lambda b, c: (0, s))
    const3 = lambda shape: pl.BlockSpec(shape, lambda b, c: (0, 0, 0))
    kernel = functools.partial(_mixer_kernel, chunk_decay=chunk_decay, slopes=_alibi_slopes())
    out = pl.pallas_call(
        kernel,
        grid=(batch, nchunks),
        in_specs=[pl.BlockSpec(memory_space=pltpu.SMEM),
                  seg(SEG_QA), seg(SEG_QR), seg(SEG_KR), seg(SEG_VR), seg(SEG_GR),
                  pl.BlockSpec((None, BLOCK, PROJ_BN), lambda b, c: (b, c, SEG_KVA)),
                  pl.BlockSpec((None, BLOCK, PROJ_BN), lambda b, c: (b, jnp.maximum(c - 1, 0), SEG_KVA)),
                  meta_seg(SEG_KR), meta_seg(SEG_VR),
                  pl.BlockSpec((BLOCK, PROJ_BN), lambda b, c: (0, SEG_KVA)),
                  const3(decay.shape), const3(k_w.shape), const3(q_w.shape)],
        out_specs=pl.BlockSpec((None, BLOCK, MIX_WIDTH), lambda b, c: (b, c, 0)),
        out_shape=jax.ShapeDtypeStruct((batch, seq, MIX_WIDTH), BF16),
        scratch_shapes=[pltpu.VMEM((RET_HEADS, RET_DIM, RET_DIM), F32)],
        compiler_params=_params("arbitrary", "arbitrary"),
        name="mixer_core",
    )(sinks.astype(F32), proj3, proj3, proj3, proj3, proj3, proj3, proj3,
      meta_chunk, meta_chunk, meta_chunk, decay, k_w, q_w)
    return out.reshape(batch * seq, MIX_WIDTH)


def _ffn_half_step(h, gain, w_gate_up, w_down):
    f = w_down.shape[0]
    hn = _rmsnorm(h, gain, BF16)
    act = _gate_up_swiglu(hn, w_gate_up)
    return _matmul_residual(act, w_down, h, FFN_HALF, f // 2 if f % (2 * LANES) == 0 else None)


def kernel(x, meta_tokens, norm_ffn1, w_ffn1_gate_up, w_ffn1_down, norm_mix, w_in, b_in, attn_sinks, w_out,
           norm_ffn2, w_ffn2_gate_up, w_ffn2_down, norm_final):
    batch, seq, d = x.shape
    depth = norm_ffn1.shape[0]
    rows = x.reshape(batch * seq, d)
    meta = meta_tokens.astype(x.dtype)
    for layer in range(depth):
        last = layer == depth - 1
        wgu1, wd1 = w_ffn1_gate_up[layer].astype(BF16), w_ffn1_down[layer].astype(BF16)
        wgu2, wd2 = w_ffn2_gate_up[layer].astype(BF16), w_ffn2_down[layer].astype(BF16)
        win, wout = w_in[layer].astype(BF16), w_out[layer].astype(BF16)

        rows = _ffn_half_step(rows, norm_ffn1[layer], wgu1, wd1)
        meta = _ffn_half_step(meta, norm_ffn1[layer], wgu1, wd1)

        proj = _in_proj(_rmsnorm(rows, norm_mix[layer], BF16), win, b_in[layer])
        proj_meta = _in_proj(_rmsnorm(meta, norm_mix[layer], BF16), win, b_in[layer])
        merged = _mixer_core(proj, proj_meta, attn_sinks[layer], batch, seq)
        rows = _matmul_residual(merged, wout, rows, 1.0, None)

        rows = _ffn_half_step(rows, norm_ffn2[layer], wgu2, wd2)
        if not last:
            raise NotImplementedError("DEPTH > 1 needs the meta rows' mixer outputs")
    out = _rmsnorm(rows, norm_final, x.dtype)
    return out.reshape(batch, seq, d)
```

```python
import functools

import numpy as np
import jax
import jax.numpy as jnp
from jax import lax
from jax.experimental import pallas as pl
from jax.experimental.pallas import tpu as pltpu

F32 = jnp.float32
BF16 = jnp.bfloat16

N_META = 16
BLOCK = 128
ATT_HEADS = 32
ATT_KV_HEADS = 4
ATT_HEAD_DIM = 64
ATT_GROUP = ATT_HEADS // ATT_KV_HEADS
ATT_WIDTH = ATT_HEADS * ATT_HEAD_DIM
ATT_KV_WIDTH = ATT_KV_HEADS * ATT_HEAD_DIM
RET_HEADS = 8
RET_DIM = 256
RET_WIDTH = RET_HEADS * RET_DIM
MIX_WIDTH = ATT_WIDTH + RET_WIDTH
IN_WIDTH = ATT_WIDTH + 2 * ATT_KV_WIDTH + 4 * RET_WIDTH
FFN_HALF = 0.5
NORM_EPS = 1e-6
MASK_VALUE = -1e30

LANES = 128
PROJ_BN = 512
ROW_BLOCK = 2048
MXU_ROWS = 1024
VMEM_LIMIT_BYTES = 56 * 1024 * 1024


def _params(*semantics):
    return pltpu.CompilerParams(dimension_semantics=semantics, vmem_limit_bytes=VMEM_LIMIT_BYTES)


def _block(total, preferred):
    if total <= preferred:
        return total
    b = preferred
    while total % b:
        b //= 2
    return b


def _lane_tile(x, width):
    return jnp.concatenate([x] * (width // LANES), axis=1)


def _resident(shape, index_map):
    return pl.BlockSpec(shape, index_map, pipeline_mode=pl.Buffered(1))


def _gain_lanes(gain):
    return jnp.broadcast_to(gain.astype(F32)[:, None], (gain.shape[0], LANES))


def _cast_stats_kernel(x_ref, xb_ref, rstd_ref):
    x = x_ref[...]
    xb_ref[...] = x.astype(xb_ref.dtype)
    rstd = lax.rsqrt(jnp.mean(x * x, axis=-1, keepdims=True) + NORM_EPS)
    rstd_ref[...] = jnp.broadcast_to(rstd, rstd_ref.shape)


def _cast_stats(x):
    m, d = x.shape
    bm = _block(m, 256)
    return pl.pallas_call(
        _cast_stats_kernel,
        grid=(m // bm,),
        in_specs=[pl.BlockSpec((bm, d), lambda i: (i, 0))],
        out_specs=[pl.BlockSpec((bm, d), lambda i: (i, 0)),
                   pl.BlockSpec((bm, LANES), lambda i: (i, 0))],
        out_shape=[jax.ShapeDtypeStruct((m, d), BF16), jax.ShapeDtypeStruct((m, LANES), F32)],
        compiler_params=_params("parallel"),
        name="cast_stats",
    )(x)


def _scale_rows_kernel(x_ref, rstd_ref, g_ref, o_ref):
    y = x_ref[...] * _lane_tile(rstd_ref[...], x_ref.shape[1])
    o_ref[...] = (y * g_ref[...]).astype(o_ref.dtype)


def _scale_rows(x, rstd, gain):
    m, d = x.shape
    bm = _block(m, 256)
    return pl.pallas_call(
        _scale_rows_kernel,
        grid=(m // bm,),
        in_specs=[pl.BlockSpec((bm, d), lambda i: (i, 0)),
                  pl.BlockSpec((bm, LANES), lambda i: (i, 0)),
                  pl.BlockSpec((1, d), lambda i: (0, 0))],
        out_specs=pl.BlockSpec((bm, d), lambda i: (i, 0)),
        out_shape=jax.ShapeDtypeStruct((m, d), x.dtype),
        compiler_params=_params("parallel"),
        name="scale_rows",
    )(x, rstd, gain.reshape(1, d).astype(F32))


def _swiglu_kernel(x_ref, rstd_ref, gain_ref, wg_ref, wu_ref, *rest, with_cast):
    if with_cast:
        side_ref, o_ref, side_out_ref = rest
        side_out_ref[...] = side_ref[...].astype(side_out_ref.dtype)
    else:
        o_ref, = rest
    bm, bf = o_ref.shape
    gain = _lane_tile(gain_ref[...], bf)
    wg = (wg_ref[...] * gain).astype(BF16)
    wu = (wu_ref[...] * gain).astype(BF16)
    rows = min(bm, MXU_ROWS)
    for r in range(0, bm, rows):
        x = x_ref[r:r + rows, :]
        scale = _lane_tile(rstd_ref[r:r + rows, :], bf)
        g = jnp.dot(x, wg, preferred_element_type=F32) * scale
        u = jnp.dot(x, wu, preferred_element_type=F32) * scale
        o_ref[r:r + rows, :] = (g * jax.nn.sigmoid(g) * u).astype(o_ref.dtype)


def _gate_up_swiglu(xb, rstd, gain, w_gate_up, to_bf16=None):
    m, d = xb.shape
    f = w_gate_up.shape[1] // 2
    bm = _block(m, ROW_BLOCK)
    bf = _block(f, 256)
    nf = f // bf
    in_specs = [_resident((bm, d), lambda i, j: (i, 0)),
                _resident((bm, LANES), lambda i, j: (i, 0)),
                _resident((d, LANES), lambda i, j: (0, 0)),
                pl.BlockSpec((d, bf), lambda i, j: (0, j)),
                pl.BlockSpec((d, bf), lambda i, j: (0, j + nf))]
    args = [xb, rstd, _gain_lanes(gain), w_gate_up, w_gate_up]
    out_specs = [pl.BlockSpec((bm, bf), lambda i, j: (i, j))]
    out_shape = [jax.ShapeDtypeStruct((m, f), BF16)]
    with_cast = to_bf16 is not None
    if with_cast:
        rows, width = to_bf16.shape
        slab = rows // ((m // bm) * nf)
        assert slab * (m // bm) * nf == rows and slab % 16 == 0, "side matrix rows must split into bf16 tiles"
        slab_spec = pl.BlockSpec((slab, width), lambda i, j: (i * nf + j, 0))
        in_specs.append(slab_spec)
        args.append(to_bf16)
        out_specs.append(slab_spec)
        out_shape.append(jax.ShapeDtypeStruct((rows, width), BF16))
    out = pl.pallas_call(
        functools.partial(_swiglu_kernel, with_cast=with_cast),
        grid=(m // bm, nf),
        in_specs=in_specs,
        out_specs=out_specs,
        out_shape=out_shape,
        compiler_params=_params("parallel", "arbitrary"),
        name="gate_up_swiglu",
    )(*args)
    return out if with_cast else out[0]


def _matmul_residual_kernel(a_ref, w_ref, r_ref, *out_refs, scale, nj, emit_bf16):
    o_ref, rstd_ref = out_refs[0], out_refs[-1]
    j = pl.program_id(1)
    width = nj * w_ref.shape[1]

    @pl.when(j == 0)
    def _zero_sums():
        rstd_ref[...] = jnp.zeros_like(rstd_ref)

    out = r_ref[...] + scale * jnp.dot(a_ref[...], w_ref[...].astype(BF16), preferred_element_type=F32)
    o_ref[...] = out
    if emit_bf16:
        out_refs[1][...] = out.astype(BF16)
    acc = rstd_ref[...] + jnp.sum(out * out, axis=-1, keepdims=True)
    rstd_ref[...] = jnp.where(j == nj - 1, lax.rsqrt(acc * (1.0 / width) + NORM_EPS), acc)


def _matmul_residual(a, w, resid, scale, bm_pref, emit_bf16):
    m, kdim = a.shape
    n = w.shape[1]
    bm = _block(m, bm_pref)
    bn = _block(n, 512)
    tile = pl.BlockSpec((bm, bn), lambda i, j: (i, j))
    out_specs = [tile] + ([tile] if emit_bf16 else []) + [pl.BlockSpec((bm, LANES), lambda i, j: (i, 0))]
    out_shape = ([jax.ShapeDtypeStruct((m, n), F32)] + ([jax.ShapeDtypeStruct((m, n), BF16)] if emit_bf16 else [])
                 + [jax.ShapeDtypeStruct((m, LANES), F32)])
    return pl.pallas_call(
        functools.partial(_matmul_residual_kernel, scale=scale, nj=n // bn, emit_bf16=emit_bf16),
        grid=(m // bm, n // bn),
        in_specs=[pl.BlockSpec((bm, kdim), lambda i, j: (i, 0)),
                  pl.BlockSpec((kdim, bn), lambda i, j: (0, j)),
                  tile],
        out_specs=out_specs,
        out_shape=out_shape,
        compiler_params=_params("parallel", "arbitrary"),
        name="matmul_residual",
    )(a, w, resid)


_QKV_BLOCKS = ATT_WIDTH // PROJ_BN
_N_PROJ_BLOCKS = IN_WIDTH // PROJ_BN
SEG_QA, SEG_QR, SEG_KR, SEG_VR, SEG_GR = 0, 1, 2, 3, 4
SEG_KVA = _N_PROJ_BLOCKS - 1


def _proj_out_block(j):
    return jnp.where(j < _QKV_BLOCKS, j, jnp.where(j == _QKV_BLOCKS, _N_PROJ_BLOCKS - 1, j - 1))


def _in_proj_kernel(x_ref, rstd_ref, gain_ref, w_ref, b_ref, o_ref):
    bm, bn = o_ref.shape
    w = (w_ref[...] * _lane_tile(gain_ref[...], bn)).astype(BF16)
    rows = min(bm, MXU_ROWS)
    for r in range(0, bm, rows):
        acc = jnp.dot(x_ref[r:r + rows, :], w, preferred_element_type=F32)
        acc = acc * _lane_tile(rstd_ref[r:r + rows, :], bn) + b_ref[...]
        o_ref[r:r + rows, :] = acc.astype(o_ref.dtype)


def _in_proj(xb, rstd, gain, w_in, b_in):
    m, d = xb.shape
    bm = _block(m, ROW_BLOCK)
    return pl.pallas_call(
        _in_proj_kernel,
        grid=(m // bm, _N_PROJ_BLOCKS),
        in_specs=[_resident((bm, d), lambda i, j: (i, 0)),
                  _resident((bm, LANES), lambda i, j: (i, 0)),
                  _resident((d, LANES), lambda i, j: (0, 0)),
                  pl.BlockSpec((d, PROJ_BN), lambda i, j: (0, j)),
                  pl.BlockSpec((1, PROJ_BN), lambda i, j: (0, j))],
        out_specs=pl.BlockSpec((bm, PROJ_BN), lambda i, j: (i, _proj_out_block(j))),
        out_shape=jax.ShapeDtypeStruct((m, IN_WIDTH), BF16),
        compiler_params=_params("parallel", "arbitrary"),
        name="in_proj",
    )(xb, rstd, _gain_lanes(gain), w_in, b_in.reshape(1, IN_WIDTH).astype(F32))


def _retention_constants():
    h = np.arange(RET_HEADS, dtype=np.float64)
    log_gamma = np.log(1.0 - 2.0 ** (-5.0 - h))
    pos = np.arange(BLOCK, dtype=np.float64)
    rel = pos[:, None] - pos[None, :]
    decay = np.where(rel >= 0, np.exp(np.maximum(rel, 0.0)[None] * log_gamma[:, None, None]), 0.0)
    k_w = np.exp((BLOCK - 1 - pos)[None, :] * log_gamma[:, None])
    q_w = np.exp((pos + 1.0)[None, :] * log_gamma[:, None])
    chunk_decay = np.exp(BLOCK * log_gamma)
    wide = lambda w: np.broadcast_to(w[:, :, None], (RET_HEADS, BLOCK, RET_DIM))
    return (jnp.asarray(decay, F32), jnp.asarray(wide(k_w), F32), jnp.asarray(wide(q_w), F32),
            [float(c) for c in chunk_decay])


def _alibi_slopes():
    return [float(2.0 ** (-8.0 * i / ATT_HEADS)) for i in range(1, ATT_HEADS + 1)]


def _lane_halves(x):
    lane = lax.broadcasted_iota(jnp.int32, x.shape, 1)
    swapped = pltpu.roll(x, ATT_HEAD_DIM, 1)
    low = lane < ATT_HEAD_DIM
    return jnp.where(low, x, swapped), jnp.where(low, swapped, x)


def _retention_state_update(k_scaled, v, kw, state_prev, chunk_decay):
    kd = (k_scaled * kw).astype(BF16)
    kv = lax.dot_general(kd, v, (((0,), (0,)), ((), ())), preferred_element_type=F32)
    if state_prev is None:
        return kv
    return state_prev * chunk_decay + kv


def _mixer_kernel(sink_ref, qa_ref, qr_ref, kr_ref, vr_ref, gr_ref, kva_ref, kva_prev_ref,
                  mkr_ref, mvr_ref, mkva_ref, decay_ref, kw_ref, qw_ref,
                  o_ref, state_ref, *, chunk_decay, slopes):
    c = pl.program_id(1)
    k_scale = RET_DIM ** -0.5

    @pl.when(c == 0)
    def _from_meta_chunk():
        for h in range(RET_HEADS):
            cols = slice(h * RET_DIM, (h + 1) * RET_DIM)
            km = mkr_ref[:, cols].astype(F32) * k_scale
            state_ref[h] = _retention_state_update(km, mvr_ref[:, cols], kw_ref[h], None, None)

    kv_prev = jnp.where(c == 0, mkva_ref[...], kva_prev_ref[...])
    kv_cat = jnp.concatenate([kv_prev, kva_ref[...]], axis=0).astype(F32)
    row = lax.broadcasted_iota(jnp.int32, (BLOCK, 2 * BLOCK), 0)
    col = lax.broadcasted_iota(jnp.int32, (BLOCK, 2 * BLOCK), 1)
    dist = BLOCK + row - col
    first_key = jnp.where(c == 0, BLOCK - N_META, 0)
    valid = (dist >= 0) & (dist < BLOCK) & (col >= first_key)
    distf = dist.astype(F32)
    lane = lax.broadcasted_iota(jnp.int32, (BLOCK, LANES), 1)
    low = lane < ATT_HEAD_DIM
    att_scale = ATT_HEAD_DIM ** -0.5

    kdup, vdup = [], []
    for t in range(ATT_KV_WIDTH // LANES):
        k_lo, k_hi = _lane_halves(kv_cat[:, t * LANES:(t + 1) * LANES])
        v_lo, v_hi = _lane_halves(kv_cat[:, ATT_KV_WIDTH + t * LANES:ATT_KV_WIDTH + (t + 1) * LANES])
        kdup += [k_lo.astype(BF16), k_hi.astype(BF16)]
        vdup += [v_lo.astype(BF16), v_hi.astype(BF16)]

    zero = jnp.zeros((BLOCK, LANES), BF16)
    for h in range(ATT_KV_HEADS):
        q_rows = []
        for p in range(ATT_GROUP // 2):
            q2 = qa_ref[:, (h * ATT_GROUP + 2 * p) * ATT_HEAD_DIM:(h * ATT_GROUP + 2 * p + 2) * ATT_HEAD_DIM]
            q_rows += [jnp.where(low, q2, zero), jnp.where(low, zero, q2)]
        q_all = jnp.concatenate(q_rows, axis=0)
        s_all = lax.dot_general(q_all, kdup[h], (((1,), (1,)), ((), ())), preferred_element_type=F32)
        probs, denoms = [], []
        for g in range(ATT_GROUP):
            head = h * ATT_GROUP + g
            s = s_all[g * BLOCK:(g + 1) * BLOCK] * att_scale
            s = jnp.where(valid, s - slopes[head] * distf, MASK_VALUE)
            sink = sink_ref[head]
            m = jnp.maximum(jnp.max(s, axis=-1, keepdims=True), sink)
            p_ = jnp.exp(s - m)
            denoms.append(jnp.sum(p_, axis=-1, keepdims=True) + jnp.exp(sink - m))
            probs.append(p_.astype(BF16))
        pv = jnp.dot(jnp.concatenate(probs, axis=0), vdup[h], preferred_element_type=F32)
        for p in range(ATT_GROUP // 2):
            even = pv[(2 * p) * BLOCK:(2 * p + 1) * BLOCK] / denoms[2 * p]
            odd = pv[(2 * p + 1) * BLOCK:(2 * p + 2) * BLOCK] / denoms[2 * p + 1]
            c0 = (h * ATT_GROUP + 2 * p) * ATT_HEAD_DIM
            o_ref[:, c0:c0 + LANES] = jnp.where(low, even, odd).astype(o_ref.dtype)

    for h in range(RET_HEADS):
        cols = slice(h * RET_DIM, (h + 1) * RET_DIM)
        q = qr_ref[:, cols]
        v = vr_ref[:, cols]
        k_scaled = kr_ref[:, cols].astype(F32) * k_scale
        s = lax.dot_general(q, k_scaled.astype(BF16), (((1,), (1,)), ((), ())), preferred_element_type=F32)
        s = s * decay_ref[h]
        inner = jnp.dot(s.astype(BF16), v, preferred_element_type=F32)
        state = state_ref[h]
        qd = (q.astype(F32) * qw_ref[h]).astype(BF16)
        cross = jnp.dot(qd, state.astype(BF16), preferred_element_type=F32)
        state_ref[h] = _retention_state_update(k_scaled, v, kw_ref[h], state, chunk_decay[h])
        ret = inner + cross
        ret = ret * lax.rsqrt(jnp.mean(ret * ret, axis=-1, keepdims=True) + NORM_EPS)
        gate = gr_ref[:, cols].astype(F32)
        o_ref[:, ATT_WIDTH + h * RET_DIM:ATT_WIDTH + (h + 1) * RET_DIM] = (
            ret * (gate * jax.nn.sigmoid(gate))).astype(o_ref.dtype)


def _mixer_core(proj, proj_meta, sinks, batch, seq):
    nchunks = seq // BLOCK
    decay, k_w, q_w, chunk_decay = _retention_constants()
    proj3 = proj.reshape(batch, seq, IN_WIDTH)
    meta_chunk = jnp.pad(proj_meta, ((BLOCK - N_META, 0), (0, 0)))
    seg = lambda s: pl.BlockSpec((None, BLOCK, ATT_WIDTH), lambda b, c: (b, c, s))
    meta_seg = lambda s: pl.BlockSpec((BLOCK, ATT_WIDTH), lambda b, c: (0, s))
    const3 = lambda shape: pl.BlockSpec(shape, lambda b, c: (0, 0, 0))
    kernel = functools.partial(_mixer_kernel, chunk_decay=chunk_decay, slopes=_alibi_slopes())
    out = pl.pallas_call(
        kernel,
        grid=(batch, nchunks),
        in_specs=[pl.BlockSpec(memory_space=pltpu.SMEM),
                  seg(SEG_QA), seg(SEG_QR), seg(SEG_KR), seg(SEG_VR), seg(SEG_GR),
                  pl.BlockSpec((None, BLOCK, PROJ_BN), lambda b, c: (b, c, SEG_KVA)),
                  pl.BlockSpec((None, BLOCK, PROJ_BN), lambda b, c: (b, jnp.maximum(c - 1, 0), SEG_KVA)),
                  meta_seg(SEG_KR), meta_seg(SEG_VR),
                  pl.BlockSpec((BLOCK, PROJ_BN), lambda b, c: (0, SEG_KVA)),
                  const3(decay.shape), const3(k_w.shape), const3(q_w.shape)],
        out_specs=pl.BlockSpec((None, BLOCK, MIX_WIDTH), lambda b, c: (b, c, 0)),
        out_shape=jax.ShapeDtypeStruct((batch, seq, MIX_WIDTH), BF16),
        scratch_shapes=[pltpu.VMEM((RET_HEADS, RET_DIM, RET_DIM), F32)],
        compiler_params=_params("arbitrary", "arbitrary"),
        name="mixer_core",
    )(sinks.astype(F32), proj3, proj3, proj3, proj3, proj3, proj3, proj3,
      meta_chunk, meta_chunk, meta_chunk, decay, k_w, q_w)
    return out.reshape(batch * seq, MIX_WIDTH)


def kernel(x, meta_tokens, norm_ffn1, w_ffn1_gate_up, w_ffn1_down, norm_mix, w_in, b_in, attn_sinks, w_out,
           norm_ffn2, w_ffn2_gate_up, w_ffn2_down, norm_final):
    batch, seq, d = x.shape
    assert norm_ffn1.shape[0] == 1, "one layer: the meta rows' mixer and FFN2 outputs are never formed"
    rows = x.reshape(batch * seq, d)

    def ffn1_and_projection(h, wd1):
        hb, rstd = _cast_stats(h)
        if wd1 is None:
            act, wd1 = _gate_up_swiglu(hb, rstd, norm_ffn1[0], w_ffn1_gate_up[0], to_bf16=w_ffn1_down[0])
        else:
            act = _gate_up_swiglu(hb, rstd, norm_ffn1[0], w_ffn1_gate_up[0])
        h1, h1b, rstd1 = _matmul_residual(act, wd1, h, FFN_HALF, 512, True)
        return h1, _in_proj(h1b, rstd1, norm_mix[0], w_in[0], b_in[0]), wd1

    rows, proj, wd1 = ffn1_and_projection(rows, None)
    _, proj_meta, _ = ffn1_and_projection(meta_tokens.astype(x.dtype), wd1)

    merged = _mixer_core(proj, proj_meta, attn_sinks[0], batch, seq)
    rows, rows_b, rows_rstd = _matmul_residual(merged, w_out[0], rows, 1.0, 1024, True)

    act, wd2 = _gate_up_swiglu(rows_b, rows_rstd, norm_ffn2[0], w_ffn2_gate_up[0], to_bf16=w_ffn2_down[0])
    rows, rows_rstd = _matmul_residual(act, wd2, rows, FFN_HALF, 512, False)
    out = _scale_rows(rows, rows_rstd, norm_final)
    return out.reshape(batch, seq, d)
```

```python
import functools

import numpy as np
import jax
import jax.numpy as jnp
from jax import lax
from jax.experimental import pallas as pl
from jax.experimental.pallas import tpu as pltpu

F32 = jnp.float32
BF16 = jnp.bfloat16

N_META = 16
BLOCK = 128
ATT_HEADS = 32
ATT_KV_HEADS = 4
ATT_HEAD_DIM = 64
ATT_GROUP = ATT_HEADS // ATT_KV_HEADS
ATT_WIDTH = ATT_HEADS * ATT_HEAD_DIM
ATT_KV_WIDTH = ATT_KV_HEADS * ATT_HEAD_DIM
RET_HEADS = 8
RET_DIM = 256
RET_WIDTH = RET_HEADS * RET_DIM
MIX_WIDTH = ATT_WIDTH + RET_WIDTH
IN_WIDTH = ATT_WIDTH + 2 * ATT_KV_WIDTH + 4 * RET_WIDTH
FFN_HALF = 0.5
NORM_EPS = 1e-6
MASK_VALUE = -1e30

LANES = 128
PROJ_BN = 512
ROW_BLOCK = 2048
MXU_ROWS = 1024
DOWN_TILE = (1024, 256)
VMEM_LIMIT_BYTES = 63 * 1024 * 1024


def _params(*semantics):
    return pltpu.CompilerParams(dimension_semantics=semantics, vmem_limit_bytes=VMEM_LIMIT_BYTES)


def _block(total, preferred):
    if total <= preferred:
        return total
    b = preferred
    while total % b:
        b //= 2
    return b


def _lane_tile(x, width):
    return jnp.concatenate([x] * (width // LANES), axis=1)


def _resident(shape, index_map):
    return pl.BlockSpec(shape, index_map, pipeline_mode=pl.Buffered(1))


def _gain_lanes(gain):
    return jnp.broadcast_to(gain.astype(F32)[:, None], (gain.shape[0], LANES))


def _cast_stats_kernel(x_ref, xb_ref, rstd_ref):
    x = x_ref[...]
    xb_ref[...] = x.astype(xb_ref.dtype)
    rstd = lax.rsqrt(jnp.mean(x * x, axis=-1, keepdims=True) + NORM_EPS)
    rstd_ref[...] = jnp.broadcast_to(rstd, rstd_ref.shape)


def _cast_stats(x):
    m, d = x.shape
    bm = _block(m, 256)
    return pl.pallas_call(
        _cast_stats_kernel,
        grid=(m // bm,),
        in_specs=[pl.BlockSpec((bm, d), lambda i: (i, 0))],
        out_specs=[pl.BlockSpec((bm, d), lambda i: (i, 0)),
                   pl.BlockSpec((bm, LANES), lambda i: (i, 0))],
        out_shape=[jax.ShapeDtypeStruct((m, d), BF16), jax.ShapeDtypeStruct((m, LANES), F32)],
        compiler_params=_params("parallel"),
        name="cast_stats",
    )(x)


def _scale_rows_kernel(x_ref, rstd_ref, g_ref, o_ref):
    y = x_ref[...] * _lane_tile(rstd_ref[...], x_ref.shape[1])
    o_ref[...] = (y * g_ref[...]).astype(o_ref.dtype)


def _scale_rows(x, rstd, gain):
    m, d = x.shape
    bm = _block(m, 256)
    return pl.pallas_call(
        _scale_rows_kernel,
        grid=(m // bm,),
        in_specs=[pl.BlockSpec((bm, d), lambda i: (i, 0)),
                  pl.BlockSpec((bm, LANES), lambda i: (i, 0)),
                  pl.BlockSpec((1, d), lambda i: (0, 0))],
        out_specs=pl.BlockSpec((bm, d), lambda i: (i, 0)),
        out_shape=jax.ShapeDtypeStruct((m, d), x.dtype),
        compiler_params=_params("parallel"),
        name="scale_rows",
    )(x, rstd, gain.reshape(1, d).astype(F32))


def _prepared_weight(w_ref, gain_ref):
    return (w_ref[...] * _lane_tile(gain_ref[...], w_ref.shape[1])).astype(BF16)


def _swiglu_kernel(x_ref, rstd_ref, gain_ref, wg_ref, wu_ref, *rest, with_cast):
    if with_cast:
        side_ref, o_ref, side_out_ref = rest
        side_out_ref[...] = side_ref[...].astype(side_out_ref.dtype)
    else:
        o_ref, = rest
    bm, bf = o_ref.shape
    wg, wu = _prepared_weight(wg_ref, gain_ref), _prepared_weight(wu_ref, gain_ref)
    rows = min(bm, MXU_ROWS)
    for r in range(0, bm, rows):
        x = x_ref[r:r + rows, :]
        scale = _lane_tile(rstd_ref[r:r + rows, :], bf)
        g = jnp.dot(x, wg, preferred_element_type=F32) * scale
        u = jnp.dot(x, wu, preferred_element_type=F32) * scale
        o_ref[r:r + rows, :] = (g * jax.nn.sigmoid(g) * u).astype(o_ref.dtype)


def _gate_up_swiglu(xb, rstd, gain, w_gate_up, to_bf16=None):
    m, d = xb.shape
    f = w_gate_up.shape[1] // 2
    bm = _block(m, ROW_BLOCK)
    bf = _block(f, 256)
    nf = f // bf
    in_specs = [_resident((bm, d), lambda i, j: (i, 0)),
                _resident((bm, LANES), lambda i, j: (i, 0)),
                _resident((d, LANES), lambda i, j: (0, 0)),
                pl.BlockSpec((d, bf), lambda i, j: (0, j)),
                pl.BlockSpec((d, bf), lambda i, j: (0, j + nf))]
    args = [xb, rstd, _gain_lanes(gain), w_gate_up, w_gate_up]
    out_specs = [pl.BlockSpec((bm, bf), lambda i, j: (i, j))]
    out_shape = [jax.ShapeDtypeStruct((m, f), BF16)]
    with_cast = to_bf16 is not None
    if with_cast:
        rows, width = to_bf16.shape
        slab = rows // ((m // bm) * nf)
        assert slab * (m // bm) * nf == rows and slab % 16 == 0, "side matrix rows must split into bf16 tiles"
        slab_spec = pl.BlockSpec((slab, width), lambda i, j: (i * nf + j, 0))
        in_specs.append(slab_spec)
        args.append(to_bf16)
        out_specs.append(slab_spec)
        out_shape.append(jax.ShapeDtypeStruct((rows, width), BF16))
    out = pl.pallas_call(
        functools.partial(_swiglu_kernel, with_cast=with_cast),
        grid=(m // bm, nf),
        in_specs=in_specs,
        out_specs=out_specs,
        out_shape=out_shape,
        compiler_params=_params("parallel", "arbitrary"),
        name="gate_up_swiglu",
    )(*args)
    return out if with_cast else out[0]


def _matmul_residual_kernel(a_ref, w_ref, r_ref, *out_refs, scale, nj, emit_bf16):
    o_ref, rstd_ref = out_refs[0], out_refs[-1]
    j = pl.program_id(1)
    width = nj * w_ref.shape[1]

    @pl.when(j == 0)
    def _zero_sums():
        rstd_ref[...] = jnp.zeros_like(rstd_ref)

    w = w_ref[...].astype(BF16)
    bm = o_ref.shape[0]
    rows = min(bm, MXU_ROWS)
    for r in range(0, bm, rows):
        chunk = slice(r, r + rows)
        out = r_ref[chunk, :] + scale * jnp.dot(a_ref[chunk, :], w, preferred_element_type=F32)
        o_ref[chunk, :] = out
        if emit_bf16:
            out_refs[1][chunk, :] = out.astype(BF16)
        acc = rstd_ref[chunk, :] + jnp.sum(out * out, axis=-1, keepdims=True)
        rstd_ref[chunk, :] = jnp.where(j == nj - 1, lax.rsqrt(acc * (1.0 / width) + NORM_EPS), acc)


def _matmul_residual(a, w, resid, scale, bm_pref, bn_pref, emit_bf16):
    m, kdim = a.shape
    n = w.shape[1]
    bm = _block(m, bm_pref)
    bn = _block(n, bn_pref)
    tile = pl.BlockSpec((bm, bn), lambda i, j: (i, j))
    out_specs = [tile] + ([tile] if emit_bf16 else []) + [pl.BlockSpec((bm, LANES), lambda i, j: (i, 0))]
    out_shape = ([jax.ShapeDtypeStruct((m, n), F32)] + ([jax.ShapeDtypeStruct((m, n), BF16)] if emit_bf16 else [])
                 + [jax.ShapeDtypeStruct((m, LANES), F32)])
    return pl.pallas_call(
        functools.partial(_matmul_residual_kernel, scale=scale, nj=n // bn, emit_bf16=emit_bf16),
        grid=(m // bm, n // bn),
        in_specs=[pl.BlockSpec((bm, kdim), lambda i, j: (i, 0)),
                  pl.BlockSpec((kdim, bn), lambda i, j: (0, j)),
                  tile],
        out_specs=out_specs,
        out_shape=out_shape,
        compiler_params=_params("parallel", "arbitrary"),
        name="matmul_residual",
    )(a, w, resid)


_QKV_BLOCKS = ATT_WIDTH // PROJ_BN
_N_PROJ_BLOCKS = IN_WIDTH // PROJ_BN
SEG_QA, SEG_QR, SEG_KR, SEG_VR, SEG_GR = 0, 1, 2, 3, 4
SEG_KVA = _N_PROJ_BLOCKS - 1


def _proj_out_block(j):
    return jnp.where(j < _QKV_BLOCKS, j, jnp.where(j == _QKV_BLOCKS, _N_PROJ_BLOCKS - 1, j - 1))


def _in_proj_kernel(x_ref, rstd_ref, gain_ref, w_ref, b_ref, o_ref):
    bm, bn = o_ref.shape
    w = _prepared_weight(w_ref, gain_ref)
    rows = min(bm, MXU_ROWS)
    for r in range(0, bm, rows):
        acc = jnp.dot(x_ref[r:r + rows, :], w, preferred_element_type=F32)
        acc = acc * _lane_tile(rstd_ref[r:r + rows, :], bn) + b_ref[...]
        o_ref[r:r + rows, :] = acc.astype(o_ref.dtype)


def _in_proj(xb, rstd, gain, w_in, b_in):
    m, d = xb.shape
    bm = _block(m, ROW_BLOCK)
    return pl.pallas_call(
        _in_proj_kernel,
        grid=(m // bm, _N_PROJ_BLOCKS),
        in_specs=[_resident((bm, d), lambda i, j: (i, 0)),
                  _resident((bm, LANES), lambda i, j: (i, 0)),
                  _resident((d, LANES), lambda i, j: (0, 0)),
                  pl.BlockSpec((d, PROJ_BN), lambda i, j: (0, j)),
                  pl.BlockSpec((1, PROJ_BN), lambda i, j: (0, j))],
        out_specs=pl.BlockSpec((bm, PROJ_BN), lambda i, j: (i, _proj_out_block(j))),
        out_shape=jax.ShapeDtypeStruct((m, IN_WIDTH), BF16),
        compiler_params=_params("parallel", "arbitrary"),
        name="in_proj",
    )(xb, rstd, _gain_lanes(gain), w_in, b_in.reshape(1, IN_WIDTH).astype(F32))


def _retention_constants():
    h = np.arange(RET_HEADS, dtype=np.float64)
    log_gamma = np.log(1.0 - 2.0 ** (-5.0 - h))
    pos = np.arange(BLOCK, dtype=np.float64)
    rel = pos[:, None] - pos[None, :]
    decay = np.where(rel >= 0, np.exp(np.maximum(rel, 0.0)[None] * log_gamma[:, None, None]), 0.0)
    k_w = np.exp((BLOCK - 1 - pos)[None, :] * log_gamma[:, None])
    q_w = np.exp((pos + 1.0)[None, :] * log_gamma[:, None])
    chunk_decay = np.exp(BLOCK * log_gamma)
    wide = lambda w: np.broadcast_to(w[:, :, None], (RET_HEADS, BLOCK, RET_DIM))
    return (jnp.asarray(decay, F32), jnp.asarray(wide(k_w), F32), jnp.asarray(wide(q_w), F32),
            [float(c) for c in chunk_decay])


def _alibi_bias():
    slopes = 2.0 ** (-8.0 * np.arange(1, ATT_HEADS + 1, dtype=np.float64) / ATT_HEADS)
    pos = np.arange(BLOCK)
    dist = np.where(pos[None, :] > pos[:, None], BLOCK, 0) + pos[:, None] - pos[None, :]
    return jnp.asarray(-slopes[:, None, None] * dist[None], F32)


def _lane_halves(x):
    lane = lax.broadcasted_iota(jnp.int32, x.shape, 1)
    swapped = pltpu.roll(x, ATT_HEAD_DIM, 1)
    low = lane < ATT_HEAD_DIM
    return jnp.where(low, x, swapped), jnp.where(low, swapped, x)


def _retention_state_update(k_scaled, v, kw, state_prev, chunk_decay):
    kd = (k_scaled * kw).astype(BF16)
    kv = lax.dot_general(kd, v, (((0,), (0,)), ((), ())), preferred_element_type=F32)
    if state_prev is None:
        return kv
    return state_prev * chunk_decay + kv


def _mixer_kernel(sink_ref, qa_ref, qr_ref, kr_ref, vr_ref, gr_ref, kva_ref, kva_prev_ref,
                  mkr_ref, mvr_ref, mkva_ref, alibi_ref, decay_ref, kw_ref, qw_ref,
                  o_ref, state_ref, *, chunk_decay):
    c = pl.program_id(0)
    batch = o_ref.shape[0]

    @pl.when(c == 0)
    def _from_meta_chunk():
        for h in range(RET_HEADS):
            cols = slice(h * RET_DIM, (h + 1) * RET_DIM)
            km = mkr_ref[:, cols].astype(F32) * RET_DIM ** -0.5
            state0 = _retention_state_update(km, mvr_ref[:, cols], kw_ref[h], None, None)
            for b in range(batch):
                state_ref[b, h] = state0

    for b in range(batch):
        _mixer_chunk(c, sink_ref, qa_ref.at[b], qr_ref.at[b], kr_ref.at[b], vr_ref.at[b], gr_ref.at[b],
                     kva_ref.at[b], kva_prev_ref.at[b], mkva_ref, alibi_ref, decay_ref, kw_ref, qw_ref,
                     o_ref.at[b], state_ref.at[b], chunk_decay)


def _mixer_chunk(c, sink_ref, qa_ref, qr_ref, kr_ref, vr_ref, gr_ref, kva_ref, kva_prev_ref,
                 mkva_ref, alibi_ref, decay_ref, kw_ref, qw_ref, o_ref, state_ref, chunk_decay):
    k_scale = RET_DIM ** -0.5

    kv_prev = jnp.where(c == 0, mkva_ref[...], kva_prev_ref[...])
    kv_cat = jnp.concatenate([kv_prev, kva_ref[...]], axis=0).astype(F32)
    row = lax.broadcasted_iota(jnp.int32, (BLOCK, BLOCK), 0)
    col = lax.broadcasted_iota(jnp.int32, (BLOCK, BLOCK), 1)
    from_prev = col > row
    masked = from_prev & (col < jnp.where(c == 0, BLOCK - N_META, 0))
    lane = lax.broadcasted_iota(jnp.int32, (BLOCK, LANES), 1)
    low = lane < ATT_HEAD_DIM

    kdup, vdup = [], []
    for t in range(ATT_KV_WIDTH // LANES):
        k_lo, k_hi = _lane_halves(kv_cat[:, t * LANES:(t + 1) * LANES])
        v_lo, v_hi = _lane_halves(kv_cat[:, ATT_KV_WIDTH + t * LANES:ATT_KV_WIDTH + (t + 1) * LANES])
        kdup += [k_lo.astype(BF16), k_hi.astype(BF16)]
        vdup += [v_lo.astype(BF16), v_hi.astype(BF16)]

    zero = jnp.zeros((BLOCK, LANES), BF16)
    for h in range(ATT_KV_HEADS):
        q_rows = []
        for p in range(ATT_GROUP // 2):
            q2 = qa_ref[:, (h * ATT_GROUP + 2 * p) * ATT_HEAD_DIM:(h * ATT_GROUP + 2 * p + 2) * ATT_HEAD_DIM]
            q2 = q2 * ATT_HEAD_DIM ** -0.5
            q_rows += [jnp.where(low, q2, zero), jnp.where(low, zero, q2)]
        q_all = jnp.concatenate(q_rows, axis=0)
        s_all = lax.dot_general(q_all, kdup[h], (((1,), (1,)), ((), ())), preferred_element_type=F32)
        probs, denoms = [], []
        for g in range(ATT_GROUP):
            head = h * ATT_GROUP + g
            s_head = s_all[g * BLOCK:(g + 1) * BLOCK]
            s = jnp.where(from_prev, s_head[:, :BLOCK], s_head[:, BLOCK:]) + alibi_ref[head]
            s = jnp.where(masked, MASK_VALUE, s)
            sink = sink_ref[head]
            m = jnp.maximum(jnp.max(s, axis=-1, keepdims=True), sink)
            p_ = jnp.exp(s - m)
            denoms.append(jnp.sum(p_, axis=-1, keepdims=True) + jnp.exp(sink - m))
            p_ = p_.astype(BF16)
            probs.append(jnp.concatenate([jnp.where(from_prev, p_, zero), jnp.where(from_prev, zero, p_)], axis=1))
        pv = jnp.dot(jnp.concatenate(probs, axis=0), vdup[h], preferred_element_type=F32)
        for p in range(ATT_GROUP // 2):
            even = pv[(2 * p) * BLOCK:(2 * p + 1) * BLOCK] / denoms[2 * p]
            odd = pv[(2 * p + 1) * BLOCK:(2 * p + 2) * BLOCK] / denoms[2 * p + 1]
            c0 = (h * ATT_GROUP + 2 * p) * ATT_HEAD_DIM
            o_ref[:, c0:c0 + LANES] = jnp.where(low, even, odd).astype(o_ref.dtype)

    for h in range(RET_HEADS):
        cols = slice(h * RET_DIM, (h + 1) * RET_DIM)
        q = qr_ref[:, cols]
        v = vr_ref[:, cols]
        k_scaled = kr_ref[:, cols].astype(F32) * k_scale
        s = lax.dot_general(q, k_scaled.astype(BF16), (((1,), (1,)), ((), ())), preferred_element_type=F32)
        s = s * decay_ref[h]
        inner = jnp.dot(s.astype(BF16), v, preferred_element_type=F32)
        state = state_ref[h]
        qd = (q.astype(F32) * qw_ref[h]).astype(BF16)
        cross = jnp.dot(qd, state.astype(BF16), preferred_element_type=F32)
        state_ref[h] = _retention_state_update(k_scaled, v, kw_ref[h], state, chunk_decay[h])
        ret = inner + cross
        ret = ret * lax.rsqrt(jnp.mean(ret * ret, axis=-1, keepdims=True) + NORM_EPS)
        gate = gr_ref[:, cols].astype(F32)
        o_ref[:, ATT_WIDTH + h * RET_DIM:ATT_WIDTH + (h + 1) * RET_DIM] = (
            ret * (gate * jax.nn.sigmoid(gate))).astype(o_ref.dtype)


def _mixer_core(proj, proj_meta, sinks, batch, seq):
    nchunks = seq // BLOCK
    decay, k_w, q_w, chunk_decay = _retention_constants()
    proj3 = proj.reshape(batch, seq, IN_WIDTH)
    meta_chunk = jnp.pad(proj_meta, ((BLOCK - N_META, 0), (0, 0)))
    seg = lambda s: pl.BlockSpec((batch, BLOCK, ATT_WIDTH), lambda c: (0, c, s))
    meta_seg = lambda s: pl.BlockSpec((BLOCK, ATT_WIDTH), lambda c: (0, s))
    const3 = lambda shape: pl.BlockSpec(shape, lambda c: (0, 0, 0))
    alibi = _alibi_bias()
    out = pl.pallas_call(
        functools.partial(_mixer_kernel, chunk_decay=chunk_decay),
        grid=(nchunks,),
        in_specs=[pl.BlockSpec(memory_space=pltpu.SMEM),
                  seg(SEG_QA), seg(SEG_QR), seg(SEG_KR), seg(SEG_VR), seg(SEG_GR),
                  pl.BlockSpec((batch, BLOCK, PROJ_BN), lambda c: (0, c, SEG_KVA)),
                  pl.BlockSpec((batch, BLOCK, PROJ_BN), lambda c: (0, jnp.maximum(c - 1, 0), SEG_KVA)),
                  meta_seg(SEG_KR), meta_seg(SEG_VR),
                  pl.BlockSpec((BLOCK, PROJ_BN), lambda c: (0, SEG_KVA)),
                  const3(alibi.shape), const3(decay.shape), const3(k_w.shape), const3(q_w.shape)],
        out_specs=pl.BlockSpec((batch, BLOCK, MIX_WIDTH), lambda c: (0, c, 0)),
        out_shape=jax.ShapeDtypeStruct((batch, seq, MIX_WIDTH), BF16),
        scratch_shapes=[pltpu.VMEM((batch, RET_HEADS, RET_DIM, RET_DIM), F32)],
        compiler_params=_params("arbitrary"),
        name="mixer_core",
    )(sinks.astype(F32), proj3, proj3, proj3, proj3, proj3, proj3, proj3,
      meta_chunk, meta_chunk, meta_chunk, alibi, decay, k_w, q_w)
    return out.reshape(batch * seq, MIX_WIDTH)


def kernel(x, meta_tokens, norm_ffn1, w_ffn1_gate_up, w_ffn1_down, norm_mix, w_in, b_in, attn_sinks, w_out,
           norm_ffn2, w_ffn2_gate_up, w_ffn2_down, norm_final):
    batch, seq, d = x.shape
    assert norm_ffn1.shape[0] == 1, "one layer: the meta rows' mixer and FFN2 outputs are never formed"
    rows = x.reshape(batch * seq, d)
    meta = meta_tokens.astype(x.dtype)

    rows_b, rows_rstd = _cast_stats(rows)
    act, wd1 = _gate_up_swiglu(rows_b, rows_rstd, norm_ffn1[0], w_ffn1_gate_up[0], to_bf16=w_ffn1_down[0])
    rows, rows_b, rows_rstd = _matmul_residual(act, wd1, rows, FFN_HALF, *DOWN_TILE, True)
    proj = _in_proj(rows_b, rows_rstd, norm_mix[0], w_in[0], b_in[0])

    meta_b, meta_rstd = _cast_stats(meta)
    act_meta = _gate_up_swiglu(meta_b, meta_rstd, norm_ffn1[0], w_ffn1_gate_up[0])
    _, meta_b, meta_rstd = _matmul_residual(act_meta, wd1, meta, FFN_HALF, *DOWN_TILE, True)
    proj_meta = _in_proj(meta_b, meta_rstd, norm_mix[0], w_in[0], b_in[0])

    merged = _mixer_core(proj, proj_meta, attn_sinks[0], batch, seq)
    rows, rows_b, rows_rstd = _matmul_residual(merged, w_out[0], rows, 1.0, ROW_BLOCK, 256, True)

    act, wd2 = _gate_up_swiglu(rows_b, rows_rstd, norm_ffn2[0], w_ffn2_gate_up[0], to_bf16=w_ffn2_down[0])
    rows, rows_rstd = _matmul_residual(act, wd2, rows, FFN_HALF, *DOWN_TILE, False)
    out = _scale_rows(rows, rows_rstd, norm_final)
    return out.reshape(batch, seq, d)
```

```python
import functools

import numpy as np
import jax
import jax.numpy as jnp
from jax import lax
from jax.experimental import pallas as pl
from jax.experimental.pallas import tpu as pltpu

F32 = jnp.float32
BF16 = jnp.bfloat16

N_META = 16
BLOCK = 128
ATT_HEADS = 32
ATT_KV_HEADS = 4
ATT_HEAD_DIM = 64
ATT_GROUP = ATT_HEADS // ATT_KV_HEADS
ATT_WIDTH = ATT_HEADS * ATT_HEAD_DIM
ATT_KV_WIDTH = ATT_KV_HEADS * ATT_HEAD_DIM
RET_HEADS = 8
RET_DIM = 256
RET_WIDTH = RET_HEADS * RET_DIM
MIX_WIDTH = ATT_WIDTH + RET_WIDTH
IN_WIDTH = ATT_WIDTH + 2 * ATT_KV_WIDTH + 4 * RET_WIDTH
FFN_HALF = 0.5
NORM_EPS = 1e-6
MASK_VALUE = -1e30

LANES = 128
PROJ_BN = 512
ROW_BLOCK = 2048
MXU_ROWS = 1024
DOWN_TILE = (1024, 256)
VMEM_LIMIT_BYTES = 63 * 1024 * 1024


def _params(*semantics):
    return pltpu.CompilerParams(dimension_semantics=semantics, vmem_limit_bytes=VMEM_LIMIT_BYTES)


def _block(total, preferred):
    if total <= preferred:
        return total
    b = preferred
    while total % b:
        b //= 2
    return b


def _lane_tile(x, width):
    return jnp.concatenate([x] * (width // LANES), axis=1)


def _resident(shape, index_map):
    return pl.BlockSpec(shape, index_map, pipeline_mode=pl.Buffered(1))


def _gain_lanes(gain):
    return jnp.broadcast_to(gain.astype(F32)[:, None], (gain.shape[0], LANES))


def _cast_stats_kernel(x_ref, xb_ref, rstd_ref):
    x = x_ref[...]
    xb_ref[...] = x.astype(xb_ref.dtype)
    rstd = lax.rsqrt(jnp.mean(x * x, axis=-1, keepdims=True) + NORM_EPS)
    rstd_ref[...] = jnp.broadcast_to(rstd, rstd_ref.shape)


def _cast_stats(x):
    m, d = x.shape
    bm = _block(m, 256)
    return pl.pallas_call(
        _cast_stats_kernel,
        grid=(m // bm,),
        in_specs=[pl.BlockSpec((bm, d), lambda i: (i, 0))],
        out_specs=[pl.BlockSpec((bm, d), lambda i: (i, 0)),
                   pl.BlockSpec((bm, LANES), lambda i: (i, 0))],
        out_shape=[jax.ShapeDtypeStruct((m, d), BF16), jax.ShapeDtypeStruct((m, LANES), F32)],
        compiler_params=_params("parallel"),
        name="cast_stats",
    )(x)


def _scale_rows_kernel(x_ref, rstd_ref, g_ref, o_ref):
    y = x_ref[...] * _lane_tile(rstd_ref[...], x_ref.shape[1])
    o_ref[...] = (y * g_ref[...]).astype(o_ref.dtype)


def _scale_rows(x, rstd, gain):
    m, d = x.shape
    bm = _block(m, 256)
    return pl.pallas_call(
        _scale_rows_kernel,
        grid=(m // bm,),
        in_specs=[pl.BlockSpec((bm, d), lambda i: (i, 0)),
                  pl.BlockSpec((bm, LANES), lambda i: (i, 0)),
                  pl.BlockSpec((1, d), lambda i: (0, 0))],
        out_specs=pl.BlockSpec((bm, d), lambda i: (i, 0)),
        out_shape=jax.ShapeDtypeStruct((m, d), x.dtype),
        compiler_params=_params("parallel"),
        name="scale_rows",
    )(x, rstd, gain.reshape(1, d).astype(F32))


def _prepared_weight(w_ref, gain_ref):
    return (w_ref[...] * _lane_tile(gain_ref[...], w_ref.shape[1])).astype(BF16)


def _swiglu_kernel(x_ref, rstd_ref, gain_ref, wg_ref, wu_ref, *rest, with_cast):
    if with_cast:
        side_ref, o_ref, side_out_ref = rest
        side_out_ref[...] = side_ref[...].astype(side_out_ref.dtype)
    else:
        o_ref, = rest
    bm, bf = o_ref.shape
    wg, wu = _prepared_weight(wg_ref, gain_ref), _prepared_weight(wu_ref, gain_ref)
    rows = min(bm, MXU_ROWS)
    for r in range(0, bm, rows):
        x = x_ref[r:r + rows, :]
        scale = _lane_tile(rstd_ref[r:r + rows, :], bf)
        g = jnp.dot(x, wg, preferred_element_type=F32) * scale
        u = jnp.dot(x, wu, preferred_element_type=F32) * scale
        o_ref[r:r + rows, :] = (g * jax.nn.sigmoid(g) * u).astype(o_ref.dtype)


def _gate_up_swiglu(xb, rstd, gain, w_gate_up, to_bf16=None):
    m, d = xb.shape
    f = w_gate_up.shape[1] // 2
    bm = _block(m, ROW_BLOCK)
    bf = _block(f, 256)
    nf = f // bf
    in_specs = [pl.BlockSpec((bm, d), lambda i, j: (i, 0)),
                _resident((bm, LANES), lambda i, j: (i, 0)),
                _resident((d, LANES), lambda i, j: (0, 0)),
                pl.BlockSpec((d, bf), lambda i, j: (0, j)),
                pl.BlockSpec((d, bf), lambda i, j: (0, j + nf))]
    args = [xb, rstd, _gain_lanes(gain), w_gate_up, w_gate_up]
    out_specs = [pl.BlockSpec((bm, bf), lambda i, j: (i, j))]
    out_shape = [jax.ShapeDtypeStruct((m, f), BF16)]
    with_cast = to_bf16 is not None
    if with_cast:
        rows, width = to_bf16.shape
        slab = rows // ((m // bm) * nf)
        assert slab * (m // bm) * nf == rows and slab % 16 == 0, "side matrix rows must split into bf16 tiles"
        slab_spec = pl.BlockSpec((slab, width), lambda i, j: (i * nf + j, 0))
        in_specs.append(slab_spec)
        args.append(to_bf16)
        out_specs.append(slab_spec)
        out_shape.append(jax.ShapeDtypeStruct((rows, width), BF16))
    out = pl.pallas_call(
        functools.partial(_swiglu_kernel, with_cast=with_cast),
        grid=(m // bm, nf),
        in_specs=in_specs,
        out_specs=out_specs,
        out_shape=out_shape,
        compiler_params=_params("parallel", "arbitrary"),
        name="gate_up_swiglu",
    )(*args)
    return out if with_cast else out[0]


def _matmul_residual_kernel(a_ref, w_ref, r_ref, *out_refs, scale, nj, emit_bf16):
    o_ref, rstd_ref = out_refs[0], out_refs[-1]
    j = pl.program_id(1)
    width = nj * w_ref.shape[1]

    @pl.when(j == 0)
    def _zero_sums():
        rstd_ref[...] = jnp.zeros_like(rstd_ref)

    w = w_ref[...].astype(BF16)
    bm = o_ref.shape[0]
    rows = min(bm, MXU_ROWS)
    for r in range(0, bm, rows):
        chunk = slice(r, r + rows)
        out = r_ref[chunk, :] + scale * jnp.dot(a_ref[chunk, :], w, preferred_element_type=F32)
        o_ref[chunk, :] = out
        if emit_bf16:
            out_refs[1][chunk, :] = out.astype(BF16)
        acc = rstd_ref[chunk, :] + jnp.sum(out * out, axis=-1, keepdims=True)
        rstd_ref[chunk, :] = jnp.where(j == nj - 1, lax.rsqrt(acc * (1.0 / width) + NORM_EPS), acc)


def _matmul_residual(a, w, resid, scale, bm_pref, bn_pref, emit_bf16):
    m, kdim = a.shape
    n = w.shape[1]
    bm = _block(m, bm_pref)
    bn = _block(n, bn_pref)
    tile = pl.BlockSpec((bm, bn), lambda i, j: (i, j))
    out_specs = [tile] + ([tile] if emit_bf16 else []) + [pl.BlockSpec((bm, LANES), lambda i, j: (i, 0))]
    out_shape = ([jax.ShapeDtypeStruct((m, n), F32)] + ([jax.ShapeDtypeStruct((m, n), BF16)] if emit_bf16 else [])
                 + [jax.ShapeDtypeStruct((m, LANES), F32)])
    return pl.pallas_call(
        functools.partial(_matmul_residual_kernel, scale=scale, nj=n // bn, emit_bf16=emit_bf16),
        grid=(m // bm, n // bn),
        in_specs=[pl.BlockSpec((bm, kdim), lambda i, j: (i, 0)),
                  pl.BlockSpec((kdim, bn), lambda i, j: (0, j)),
                  tile],
        out_specs=out_specs,
        out_shape=out_shape,
        compiler_params=_params("parallel", "arbitrary"),
        name="matmul_residual",
    )(a, w, resid)


_QKV_BLOCKS = ATT_WIDTH // PROJ_BN
_N_PROJ_BLOCKS = IN_WIDTH // PROJ_BN
SEG_QA, SEG_QR, SEG_KR, SEG_VR, SEG_GR = 0, 1, 2, 3, 4
SEG_KVA = _N_PROJ_BLOCKS - 1


def _proj_out_block(j):
    return jnp.where(j < _QKV_BLOCKS, j, jnp.where(j == _QKV_BLOCKS, _N_PROJ_BLOCKS - 1, j - 1))


META_KR, META_VR = 0, 1
_META_PROJ_BLOCKS = 2 * RET_WIDTH // PROJ_BN + 1
META_KVA = _META_PROJ_BLOCKS - 1
META_PROJ_WIDTH = _META_PROJ_BLOCKS * PROJ_BN
_KR_FIRST_BLOCK = (ATT_WIDTH + 2 * ATT_KV_WIDTH + RET_WIDTH) // PROJ_BN


def _meta_proj_in_block(j):
    return jnp.where(j < META_KVA, j + _KR_FIRST_BLOCK, _QKV_BLOCKS)


def _in_proj_kernel(x_ref, rstd_ref, gain_ref, w_ref, b_ref, o_ref):
    bm, bn = o_ref.shape
    w = _prepared_weight(w_ref, gain_ref)
    rows = min(bm, MXU_ROWS)
    for r in range(0, bm, rows):
        acc = jnp.dot(x_ref[r:r + rows, :], w, preferred_element_type=F32)
        acc = acc * _lane_tile(rstd_ref[r:r + rows, :], bn) + b_ref[...]
        o_ref[r:r + rows, :] = acc.astype(o_ref.dtype)


def _in_proj(xb, rstd, gain, w_in, b_in, keys_values_only=False):
    m, d = xb.shape
    bm = _block(m, ROW_BLOCK)
    if keys_values_only:
        n_blocks, width, in_block, out_block = _META_PROJ_BLOCKS, META_PROJ_WIDTH, _meta_proj_in_block, lambda j: j
    else:
        n_blocks, width, in_block, out_block = _N_PROJ_BLOCKS, IN_WIDTH, lambda j: j, _proj_out_block
    return pl.pallas_call(
        _in_proj_kernel,
        grid=(m // bm, n_blocks),
        in_specs=[pl.BlockSpec((bm, d), lambda i, j: (i, 0)),
                  _resident((bm, LANES), lambda i, j: (i, 0)),
                  _resident((d, LANES), lambda i, j: (0, 0)),
                  pl.BlockSpec((d, PROJ_BN), lambda i, j: (0, in_block(j))),
                  pl.BlockSpec((1, PROJ_BN), lambda i, j: (0, in_block(j)))],
        out_specs=pl.BlockSpec((bm, PROJ_BN), lambda i, j: (i, out_block(j))),
        out_shape=jax.ShapeDtypeStruct((m, width), BF16),
        compiler_params=_params("parallel", "arbitrary"),
        name="in_proj",
    )(xb, rstd, _gain_lanes(gain), w_in, b_in.reshape(1, IN_WIDTH).astype(F32))


def _retention_constants():
    h = np.arange(RET_HEADS, dtype=np.float64)
    log_gamma = np.log(1.0 - 2.0 ** (-5.0 - h))
    pos = np.arange(BLOCK, dtype=np.float64)
    rel = pos[:, None] - pos[None, :]
    decay = np.where(rel >= 0, np.exp(np.maximum(rel, 0.0)[None] * log_gamma[:, None, None]), 0.0)
    k_w = np.exp((BLOCK - 1 - pos)[None, :] * log_gamma[:, None])
    q_w = np.exp((pos + 1.0)[None, :] * log_gamma[:, None])
    chunk_decay = np.exp(BLOCK * log_gamma)
    wide = lambda w: np.broadcast_to(w[:, :, None], (RET_HEADS, BLOCK, RET_DIM))
    return (jnp.asarray(decay, F32), jnp.asarray(wide(k_w), F32), jnp.asarray(wide(q_w), F32),
            [float(c) for c in chunk_decay])


def _alibi_bias():
    slopes = 2.0 ** (-8.0 * np.arange(1, ATT_HEADS + 1, dtype=np.float64) / ATT_HEADS)
    pos = np.arange(BLOCK)
    dist = np.where(pos[None, :] > pos[:, None], BLOCK, 0) + pos[:, None] - pos[None, :]
    return jnp.asarray(-slopes[:, None, None] * dist[None], F32)


def _lane_halves(x):
    lane = lax.broadcasted_iota(jnp.int32, x.shape, 1)
    swapped = pltpu.roll(x, ATT_HEAD_DIM, 1)
    low = lane < ATT_HEAD_DIM
    return jnp.where(low, x, swapped), jnp.where(low, swapped, x)


def _retention_state_update(k_scaled, v, kw, state_prev, chunk_decay):
    kd = (k_scaled * kw).astype(BF16)
    kv = lax.dot_general(kd, v, (((0,), (0,)), ((), ())), preferred_element_type=F32)
    if state_prev is None:
        return kv
    return state_prev * chunk_decay + kv


def _mixer_kernel(sink_ref, qa_ref, qr_ref, kr_ref, vr_ref, gr_ref, kva_ref, kva_prev_ref,
                  mkr_ref, mvr_ref, mkva_ref, alibi_ref, decay_ref, kw_ref, qw_ref,
                  o_ref, state_ref, *, chunk_decay):
    c = pl.program_id(0)
    batch = o_ref.shape[0]

    @pl.when(c == 0)
    def _from_meta_chunk():
        for h in range(RET_HEADS):
            cols = slice(h * RET_DIM, (h + 1) * RET_DIM)
            km = mkr_ref[:, cols].astype(F32) * RET_DIM ** -0.5
            state0 = _retention_state_update(km, mvr_ref[:, cols], kw_ref[h], None, None)
            for b in range(batch):
                state_ref[b, h] = state0

    for b in range(batch):
        _mixer_chunk(c, sink_ref, qa_ref.at[b], qr_ref.at[b], kr_ref.at[b], vr_ref.at[b], gr_ref.at[b],
                     kva_ref.at[b], kva_prev_ref.at[b], mkva_ref, alibi_ref, decay_ref, kw_ref, qw_ref,
                     o_ref.at[b], state_ref.at[b], chunk_decay)


def _mixer_chunk(c, sink_ref, qa_ref, qr_ref, kr_ref, vr_ref, gr_ref, kva_ref, kva_prev_ref,
                 mkva_ref, alibi_ref, decay_ref, kw_ref, qw_ref, o_ref, state_ref, chunk_decay):
    k_scale = RET_DIM ** -0.5

    kv_prev = jnp.where(c == 0, mkva_ref[...], kva_prev_ref[...])
    kv_cat = jnp.concatenate([kv_prev, kva_ref[...]], axis=0).astype(F32)
    row = lax.broadcasted_iota(jnp.int32, (BLOCK, BLOCK), 0)
    col = lax.broadcasted_iota(jnp.int32, (BLOCK, BLOCK), 1)
    from_prev = col > row
    masked = from_prev & (col < jnp.where(c == 0, BLOCK - N_META, 0))
    lane = lax.broadcasted_iota(jnp.int32, (BLOCK, LANES), 1)
    low = lane < ATT_HEAD_DIM

    kdup, vdup = [], []
    for t in range(ATT_KV_WIDTH // LANES):
        k_lo, k_hi = _lane_halves(kv_cat[:, t * LANES:(t + 1) * LANES])
        v_lo, v_hi = _lane_halves(kv_cat[:, ATT_KV_WIDTH + t * LANES:ATT_KV_WIDTH + (t + 1) * LANES])
        kdup += [k_lo.astype(BF16), k_hi.astype(BF16)]
        vdup += [v_lo.astype(BF16), v_hi.astype(BF16)]

    zero = jnp.zeros((BLOCK, LANES), BF16)
    for h in range(ATT_KV_HEADS):
        q_rows = []
        for p in range(ATT_GROUP // 2):
            q2 = qa_ref[:, (h * ATT_GROUP + 2 * p) * ATT_HEAD_DIM:(h * ATT_GROUP + 2 * p + 2) * ATT_HEAD_DIM]
            q2 = q2 * ATT_HEAD_DIM ** -0.5
            q_rows += [jnp.where(low, q2, zero), jnp.where(low, zero, q2)]
        q_all = jnp.concatenate(q_rows, axis=0)
        s_all = lax.dot_general(q_all, kdup[h], (((1,), (1,)), ((), ())), preferred_element_type=F32)
        probs, denoms = [], []
        for g in range(ATT_GROUP):
            head = h * ATT_GROUP + g
            s_head = s_all[g * BLOCK:(g + 1) * BLOCK]
            s = jnp.where(from_prev, s_head[:, :BLOCK], s_head[:, BLOCK:]) + alibi_ref[head]
            s = jnp.where(masked, MASK_VALUE, s)
            sink = sink_ref[head]
            m = jnp.maximum(jnp.max(s, axis=-1, keepdims=True), sink)
            p_ = jnp.exp(s - m)
            denoms.append(jnp.sum(p_, axis=-1, keepdims=True) + jnp.exp(sink - m))
            p_ = p_.astype(BF16)
            probs.append(jnp.concatenate([jnp.where(from_prev, p_, zero), jnp.where(from_prev, zero, p_)], axis=1))
        pv = jnp.dot(jnp.concatenate(probs, axis=0), vdup[h], preferred_element_type=F32)
        for p in range(ATT_GROUP // 2):
            even = pv[(2 * p) * BLOCK:(2 * p + 1) * BLOCK] / denoms[2 * p]
            odd = pv[(2 * p + 1) * BLOCK:(2 * p + 2) * BLOCK] / denoms[2 * p + 1]
            c0 = (h * ATT_GROUP + 2 * p) * ATT_HEAD_DIM
            o_ref[:, c0:c0 + LANES] = jnp.where(low, even, odd).astype(o_ref.dtype)

    for h in range(RET_HEADS):
        cols = slice(h * RET_DIM, (h + 1) * RET_DIM)
        q = qr_ref[:, cols]
        v = vr_ref[:, cols]
        k_scaled = kr_ref[:, cols].astype(F32) * k_scale
        s = lax.dot_general(q, k_scaled.astype(BF16), (((1,), (1,)), ((), ())), preferred_element_type=F32)
        s = s * decay_ref[h]
        inner = jnp.dot(s.astype(BF16), v, preferred_element_type=F32)
        state = state_ref[h]
        qd = (q.astype(F32) * qw_ref[h]).astype(BF16)
        cross = jnp.dot(qd, state.astype(BF16), preferred_element_type=F32)
        state_ref[h] = _retention_state_update(k_scaled, v, kw_ref[h], state, chunk_decay[h])
        ret = inner + cross
        ret = ret * lax.rsqrt(jnp.mean(ret * ret, axis=-1, keepdims=True) + NORM_EPS)
        gate = gr_ref[:, cols].astype(F32)
        o_ref[:, ATT_WIDTH + h * RET_DIM:ATT_WIDTH + (h + 1) * RET_DIM] = (
            ret * (gate * jax.nn.sigmoid(gate))).astype(o_ref.dtype)


def _mixer_core(proj, proj_meta, sinks, batch, seq):
    nchunks = seq // BLOCK
    decay, k_w, q_w, chunk_decay = _retention_constants()
    proj3 = proj.reshape(batch, seq, IN_WIDTH)
    meta_chunk = jnp.pad(proj_meta, ((BLOCK - N_META, 0), (0, 0)))
    seg = lambda s: pl.BlockSpec((batch, BLOCK, ATT_WIDTH), lambda c: (0, c, s))
    meta_seg = lambda s: pl.BlockSpec((BLOCK, ATT_WIDTH), lambda c: (0, s))
    const3 = lambda shape: pl.BlockSpec(shape, lambda c: (0, 0, 0))
    alibi = _alibi_bias()
    out = pl.pallas_call(
        functools.partial(_mixer_kernel, chunk_decay=chunk_decay),
        grid=(nchunks,),
        in_specs=[pl.BlockSpec(memory_space=pltpu.SMEM),
                  seg(SEG_QA), seg(SEG_QR), seg(SEG_KR), seg(SEG_VR), seg(SEG_GR),
                  pl.BlockSpec((batch, BLOCK, PROJ_BN), lambda c: (0, c, SEG_KVA)),
                  pl.BlockSpec((batch, BLOCK, PROJ_BN), lambda c: (0, jnp.maximum(c - 1, 0), SEG_KVA)),
                  meta_seg(META_KR), meta_seg(META_VR),
                  pl.BlockSpec((BLOCK, PROJ_BN), lambda c: (0, META_KVA)),
                  const3(alibi.shape), const3(decay.shape), const3(k_w.shape), const3(q_w.shape)],
        out_specs=pl.BlockSpec((batch, BLOCK, MIX_WIDTH), lambda c: (0, c, 0)),
        out_shape=jax.ShapeDtypeStruct((batch, seq, MIX_WIDTH), BF16),
        scratch_shapes=[pltpu.VMEM((batch, RET_HEADS, RET_DIM, RET_DIM), F32)],
        compiler_params=_params("arbitrary"),
        name="mixer_core",
    )(sinks.astype(F32), proj3, proj3, proj3, proj3, proj3, proj3, proj3,
      meta_chunk, meta_chunk, meta_chunk, alibi, decay, k_w, q_w)
    return out.reshape(batch * seq, MIX_WIDTH)


def kernel(x, meta_tokens, norm_ffn1, w_ffn1_gate_up, w_ffn1_down, norm_mix, w_in, b_in, attn_sinks, w_out,
           norm_ffn2, w_ffn2_gate_up, w_ffn2_down, norm_final):
    batch, seq, d = x.shape
    assert norm_ffn1.shape[0] == 1, "one layer: the meta rows' mixer and FFN2 outputs are never formed"
    rows = x.reshape(batch * seq, d)
    meta = meta_tokens.astype(x.dtype)

    rows_b, rows_rstd = _cast_stats(rows)
    act, wd1 = _gate_up_swiglu(rows_b, rows_rstd, norm_ffn1[0], w_ffn1_gate_up[0], to_bf16=w_ffn1_down[0])
    rows, rows_b, rows_rstd = _matmul_residual(act, wd1, rows, FFN_HALF, *DOWN_TILE, True)
    proj = _in_proj(rows_b, rows_rstd, norm_mix[0], w_in[0], b_in[0])

    meta_b, meta_rstd = _cast_stats(meta)
    act_meta = _gate_up_swiglu(meta_b, meta_rstd, norm_ffn1[0], w_ffn1_gate_up[0])
    _, meta_b, meta_rstd = _matmul_residual(act_meta, wd1, meta, FFN_HALF, *DOWN_TILE, True)
    proj_meta = _in_proj(meta_b, meta_rstd, norm_mix[0], w_in[0], b_in[0], keys_values_only=True)

    merged = _mixer_core(proj, proj_meta, attn_sinks[0], batch, seq)
    rows, rows_b, rows_rstd = _matmul_residual(merged, w_out[0], rows, 1.0, ROW_BLOCK, 256, True)

    act, wd2 = _gate_up_swiglu(rows_b, rows_rstd, norm_ffn2[0], w_ffn2_gate_up[0], to_bf16=w_ffn2_down[0])
    rows, rows_rstd = _matmul_residual(act, wd2, rows, FFN_HALF, *DOWN_TILE, False)
    out = _scale_rows(rows, rows_rstd, norm_final)
    return out.reshape(batch, seq, d)
```

```python
import functools

import numpy as np
import jax
import jax.numpy as jnp
from jax import lax
from jax.experimental import pallas as pl
from jax.experimental.pallas import tpu as pltpu

F32 = jnp.float32
BF16 = jnp.bfloat16

N_META = 16
BLOCK = 128
ATT_HEADS = 32
ATT_KV_HEADS = 4
ATT_HEAD_DIM = 64
ATT_GROUP = ATT_HEADS // ATT_KV_HEADS
ATT_WIDTH = ATT_HEADS * ATT_HEAD_DIM
ATT_KV_WIDTH = ATT_KV_HEADS * ATT_HEAD_DIM
RET_HEADS = 8
RET_DIM = 256
RET_WIDTH = RET_HEADS * RET_DIM
MIX_WIDTH = ATT_WIDTH + RET_WIDTH
IN_WIDTH = ATT_WIDTH + 2 * ATT_KV_WIDTH + 4 * RET_WIDTH
FFN_HALF = 0.5
NORM_EPS = 1e-6
MASK_VALUE = -1e30

LANES = 128
MXU_COLS = 256
MXU_ROWS = 1024
PROJ_BN = 2 * MXU_COLS
ROW_BLOCK = 2 * MXU_ROWS
DOWN_TILE = (MXU_ROWS, MXU_COLS)
ELEMENTWISE_ROWS = 256
VMEM_LIMIT_BYTES = 63 * 1024 * 1024


def _params(*semantics):
    return pltpu.CompilerParams(dimension_semantics=semantics, vmem_limit_bytes=VMEM_LIMIT_BYTES)


def _block(total, preferred):
    if total <= preferred:
        return total
    b = preferred
    while total % b:
        b //= 2
    return b


def _lane_tile(x, width):
    return jnp.concatenate([x] * (width // LANES), axis=1)


def _lane_partial_sums(x):
    return functools.reduce(jnp.add, [x[:, t:t + LANES] for t in range(0, x.shape[1], LANES)])


def _row_scale(ssq, width, out_cols):
    total = jnp.sum(ssq, axis=-1, keepdims=True)
    rstd = lax.rsqrt(total * (1.0 / width) + NORM_EPS)
    return jnp.broadcast_to(rstd, (ssq.shape[0], out_cols))


def _row_scales_once(ssq_ref, rstd_ref, width):
    @pl.when(pl.program_id(1) == 0)
    def _():
        rstd_ref[...] = _row_scale(ssq_ref[...], width, LANES)


def _resident(shape, index_map):
    return pl.BlockSpec(shape, index_map, pipeline_mode=pl.Buffered(1))


def _gain_lanes(gain):
    return jnp.broadcast_to(gain.astype(F32)[:, None], (gain.shape[0], LANES))


def _cast_stats_kernel(x_ref, xb_ref, ssq_ref):
    x = x_ref[...]
    xb_ref[...] = x.astype(xb_ref.dtype)
    ssq_ref[...] = _lane_partial_sums(x * x)


def _cast_stats(x):
    m, d = x.shape
    bm = _block(m, ELEMENTWISE_ROWS)
    return pl.pallas_call(
        _cast_stats_kernel,
        grid=(m // bm,),
        in_specs=[pl.BlockSpec((bm, d), lambda i: (i, 0))],
        out_specs=[pl.BlockSpec((bm, d), lambda i: (i, 0)),
                   pl.BlockSpec((bm, LANES), lambda i: (i, 0))],
        out_shape=[jax.ShapeDtypeStruct((m, d), BF16), jax.ShapeDtypeStruct((m, LANES), F32)],
        compiler_params=_params("parallel"),
        name="cast_stats",
    )(x)


def _scale_rows_kernel(x_ref, ssq_ref, g_ref, o_ref):
    width = x_ref.shape[1]
    y = x_ref[...] * _row_scale(ssq_ref[...], width, width)
    o_ref[...] = (y * g_ref[...]).astype(o_ref.dtype)


def _scale_rows(x, ssq, gain):
    m, d = x.shape
    bm = _block(m, ELEMENTWISE_ROWS)
    return pl.pallas_call(
        _scale_rows_kernel,
        grid=(m // bm,),
        in_specs=[pl.BlockSpec((bm, d), lambda i: (i, 0)),
                  pl.BlockSpec((bm, LANES), lambda i: (i, 0)),
                  pl.BlockSpec((1, d), lambda i: (0, 0))],
        out_specs=pl.BlockSpec((bm, d), lambda i: (i, 0)),
        out_shape=jax.ShapeDtypeStruct((m, d), x.dtype),
        compiler_params=_params("parallel"),
        name="scale_rows",
    )(x, ssq, gain.reshape(1, d).astype(F32))


def _prepared_weight(w_ref, gain_ref):
    return (w_ref[...] * _lane_tile(gain_ref[...], w_ref.shape[1])).astype(BF16)


def _swiglu_kernel(x_ref, ssq_ref, gain_ref, wg_ref, wu_ref, *rest, with_cast):
    rstd_ref = rest[-1]
    if with_cast:
        side_ref, o_ref, side_out_ref = rest[:-1]
        side_out_ref[...] = side_ref[...].astype(side_out_ref.dtype)
    else:
        o_ref, = rest[:-1]
    _row_scales_once(ssq_ref, rstd_ref, x_ref.shape[1])
    bm, bf = o_ref.shape
    wg, wu = _prepared_weight(wg_ref, gain_ref), _prepared_weight(wu_ref, gain_ref)
    rows = min(bm, MXU_ROWS)
    for r in range(0, bm, rows):
        x = x_ref[r:r + rows, :]
        scale = _lane_tile(rstd_ref[r:r + rows, :], bf)
        g = jnp.dot(x, wg, preferred_element_type=F32) * scale
        u = jnp.dot(x, wu, preferred_element_type=F32) * scale
        o_ref[r:r + rows, :] = (g * jax.nn.sigmoid(g) * u).astype(o_ref.dtype)


def _gate_up_swiglu(xb, ssq, gain, w_gate_up, to_bf16=None):
    m, d = xb.shape
    f = w_gate_up.shape[1] // 2
    bm = _block(m, ROW_BLOCK)
    bf = _block(f, MXU_COLS)
    nf = f // bf
    in_specs = [pl.BlockSpec((bm, d), lambda i, j: (i, 0)),
                _resident((bm, LANES), lambda i, j: (i, 0)),
                _resident((d, LANES), lambda i, j: (0, 0)),
                pl.BlockSpec((d, bf), lambda i, j: (0, j)),
                pl.BlockSpec((d, bf), lambda i, j: (0, j + nf))]
    args = [xb, ssq, _gain_lanes(gain), w_gate_up, w_gate_up]
    out_specs = [pl.BlockSpec((bm, bf), lambda i, j: (i, j))]
    out_shape = [jax.ShapeDtypeStruct((m, f), BF16)]
    with_cast = to_bf16 is not None
    if with_cast:
        rows, width = to_bf16.shape
        slab = rows // ((m // bm) * nf)
        assert slab * (m // bm) * nf == rows and slab % 16 == 0, "side matrix rows must split into bf16 tiles"
        slab_spec = pl.BlockSpec((slab, width), lambda i, j: (i * nf + j, 0))
        in_specs.append(slab_spec)
        args.append(to_bf16)
        out_specs.append(slab_spec)
        out_shape.append(jax.ShapeDtypeStruct((rows, width), BF16))
    out = pl.pallas_call(
        functools.partial(_swiglu_kernel, with_cast=with_cast),
        grid=(m // bm, nf),
        in_specs=in_specs,
        out_specs=out_specs,
        out_shape=out_shape,
        scratch_shapes=[pltpu.VMEM((bm, LANES), F32)],
        compiler_params=_params("parallel", "arbitrary"),
        name="gate_up_swiglu",
    )(*args)
    return out if with_cast else out[0]


def _matmul_residual_kernel(a_ref, w_ref, r_ref, *out_refs, scale, emit_bf16):
    o_ref, ssq_ref = out_refs[0], out_refs[-1]

    @pl.when(pl.program_id(1) == 0)
    def _zero_sums():
        ssq_ref[...] = jnp.zeros_like(ssq_ref)

    w = w_ref[...].astype(BF16)
    bm = o_ref.shape[0]
    rows = min(bm, MXU_ROWS)
    for r in range(0, bm, rows):
        chunk = slice(r, r + rows)
        out = r_ref[chunk, :] + scale * jnp.dot(a_ref[chunk, :], w, preferred_element_type=F32)
        o_ref[chunk, :] = out
        if emit_bf16:
            out_refs[1][chunk, :] = out.astype(BF16)
        ssq_ref[chunk, :] += _lane_partial_sums(out * out)


def _matmul_residual(a, w, resid, scale, bm_pref, bn_pref, emit_bf16):
    m, kdim = a.shape
    n = w.shape[1]
    bm = _block(m, bm_pref)
    bn = _block(n, bn_pref)
    tile = pl.BlockSpec((bm, bn), lambda i, j: (i, j))
    out_specs = [tile] + ([tile] if emit_bf16 else []) + [pl.BlockSpec((bm, LANES), lambda i, j: (i, 0))]
    out_shape = ([jax.ShapeDtypeStruct((m, n), F32)] + ([jax.ShapeDtypeStruct((m, n), BF16)] if emit_bf16 else [])
                 + [jax.ShapeDtypeStruct((m, LANES), F32)])
    return pl.pallas_call(
        functools.partial(_matmul_residual_kernel, scale=scale, emit_bf16=emit_bf16),
        grid=(m // bm, n // bn),
        in_specs=[pl.BlockSpec((bm, kdim), lambda i, j: (i, 0)),
                  pl.BlockSpec((kdim, bn), lambda i, j: (0, j)),
                  tile],
        out_specs=out_specs,
        out_shape=out_shape,
        compiler_params=_params("parallel", "arbitrary"),
        name="matmul_residual",
    )(a, w, resid)


_QKV_BLOCKS = ATT_WIDTH // PROJ_BN
_N_PROJ_BLOCKS = IN_WIDTH // PROJ_BN
SEG_QA, SEG_QR, SEG_KR, SEG_VR, SEG_GR = 0, 1, 2, 3, 4
SEG_KVA = _N_PROJ_BLOCKS - 1


def _proj_out_block(j):
    return jnp.where(j < _QKV_BLOCKS, j, jnp.where(j == _QKV_BLOCKS, _N_PROJ_BLOCKS - 1, j - 1))


META_KR, META_VR = 0, 1
_META_PROJ_BLOCKS = 2 * RET_WIDTH // PROJ_BN + 1
META_KVA = _META_PROJ_BLOCKS - 1
META_PROJ_WIDTH = _META_PROJ_BLOCKS * PROJ_BN
_KR_FIRST_BLOCK = (ATT_WIDTH + 2 * ATT_KV_WIDTH + RET_WIDTH) // PROJ_BN


def _meta_proj_in_block(j):
    return jnp.where(j < META_KVA, j + _KR_FIRST_BLOCK, _QKV_BLOCKS)


def _in_proj_kernel(x_ref, ssq_ref, gain_ref, w_ref, b_ref, o_ref, rstd_ref):
    _row_scales_once(ssq_ref, rstd_ref, x_ref.shape[1])
    bm, bn = o_ref.shape
    w = _prepared_weight(w_ref, gain_ref)
    rows = min(bm, MXU_ROWS)
    for r in range(0, bm, rows):
        acc = jnp.dot(x_ref[r:r + rows, :], w, preferred_element_type=F32)
        acc = acc * _lane_tile(rstd_ref[r:r + rows, :], bn) + b_ref[...]
        o_ref[r:r + rows, :] = acc.astype(o_ref.dtype)


def _in_proj(xb, ssq, gain, w_in, b_in, keys_values_only=False):
    m, d = xb.shape
    bm = _block(m, ROW_BLOCK)
    if keys_values_only:
        n_blocks, width, in_block, out_block = _META_PROJ_BLOCKS, META_PROJ_WIDTH, _meta_proj_in_block, lambda j: j
    else:
        n_blocks, width, in_block, out_block = _N_PROJ_BLOCKS, IN_WIDTH, lambda j: j, _proj_out_block
    return pl.pallas_call(
        _in_proj_kernel,
        grid=(m // bm, n_blocks),
        in_specs=[pl.BlockSpec((bm, d), lambda i, j: (i, 0)),
                  _resident((bm, LANES), lambda i, j: (i, 0)),
                  _resident((d, LANES), lambda i, j: (0, 0)),
                  pl.BlockSpec((d, PROJ_BN), lambda i, j: (0, in_block(j))),
                  pl.BlockSpec((1, PROJ_BN), lambda i, j: (0, in_block(j)))],
        out_specs=pl.BlockSpec((bm, PROJ_BN), lambda i, j: (i, out_block(j))),
        out_shape=jax.ShapeDtypeStruct((m, width), BF16),
        scratch_shapes=[pltpu.VMEM((bm, LANES), F32)],
        compiler_params=_params("parallel", "arbitrary"),
        name="in_proj",
    )(xb, ssq, _gain_lanes(gain), w_in, b_in.reshape(1, IN_WIDTH).astype(F32))


def _retention_constants():
    h = np.arange(RET_HEADS, dtype=np.float64)
    log_gamma = np.log(1.0 - 2.0 ** (-5.0 - h))
    pos = np.arange(BLOCK, dtype=np.float64)
    rel = pos[:, None] - pos[None, :]
    decay = np.where(rel >= 0, np.exp(np.maximum(rel, 0.0)[None] * log_gamma[:, None, None]), 0.0)
    k_w = np.exp((BLOCK - 1 - pos)[None, :] * log_gamma[:, None])
    q_w = np.exp((pos + 1.0)[None, :] * log_gamma[:, None])
    chunk_decay = np.exp(BLOCK * log_gamma)
    wide = lambda w: np.broadcast_to(w[:, :, None], (RET_HEADS, BLOCK, RET_DIM))
    return (jnp.asarray(decay, F32), jnp.asarray(wide(k_w), F32), jnp.asarray(wide(q_w), F32),
            [float(c) for c in chunk_decay])


def _alibi_bias():
    slopes = 2.0 ** (-8.0 * np.arange(1, ATT_HEADS + 1, dtype=np.float64) / ATT_HEADS)
    pos = np.arange(BLOCK)
    dist = np.where(pos[None, :] > pos[:, None], BLOCK, 0) + pos[:, None] - pos[None, :]
    return jnp.asarray(-slopes[:, None, None] * dist[None], F32)


def _lane_halves(x):
    lane = lax.broadcasted_iota(jnp.int32, x.shape, 1)
    swapped = pltpu.roll(x, ATT_HEAD_DIM, 1)
    low = lane < ATT_HEAD_DIM
    return jnp.where(low, x, swapped), jnp.where(low, swapped, x)


def _retention_state_update(k_scaled, v, kw, state_prev, chunk_decay):
    kd = (k_scaled * kw).astype(BF16)
    kv = lax.dot_general(kd, v, (((0,), (0,)), ((), ())), preferred_element_type=F32)
    if state_prev is None:
        return kv
    return state_prev * chunk_decay + kv


def _mixer_kernel(sink_ref, qa_ref, qr_ref, kr_ref, vr_ref, gr_ref, kva_ref, kva_prev_ref,
                  mkr_ref, mvr_ref, mkva_ref, alibi_ref, decay_ref, kw_ref, qw_ref,
                  o_ref, state_ref, *, chunk_decay):
    c = pl.program_id(0)
    batch = o_ref.shape[0]

    @pl.when(c == 0)
    def _from_meta_chunk():
        for h in range(RET_HEADS):
            cols = slice(h * RET_DIM, (h + 1) * RET_DIM)
            km = mkr_ref[:, cols].astype(F32) * RET_DIM ** -0.5
            state0 = _retention_state_update(km, mvr_ref[:, cols], kw_ref[h], None, None)
            for b in range(batch):
                state_ref[b, h] = state0

    for b in range(batch):
        _mixer_chunk(c, sink_ref, qa_ref.at[b], qr_ref.at[b], kr_ref.at[b], vr_ref.at[b], gr_ref.at[b],
                     kva_ref.at[b], kva_prev_ref.at[b], mkva_ref, alibi_ref, decay_ref, kw_ref, qw_ref,
                     o_ref.at[b], state_ref.at[b], chunk_decay)


def _mixer_chunk(c, sink_ref, qa_ref, qr_ref, kr_ref, vr_ref, gr_ref, kva_ref, kva_prev_ref,
                 mkva_ref, alibi_ref, decay_ref, kw_ref, qw_ref, o_ref, state_ref, chunk_decay):
    k_scale = RET_DIM ** -0.5

    kv_prev = jnp.where(c == 0, mkva_ref[...], kva_prev_ref[...])
    kv_cat = jnp.concatenate([kv_prev, kva_ref[...]], axis=0).astype(F32)
    row = lax.broadcasted_iota(jnp.int32, (BLOCK, BLOCK), 0)
    col = lax.broadcasted_iota(jnp.int32, (BLOCK, BLOCK), 1)
    from_prev = col > row
    masked = from_prev & (col < jnp.where(c == 0, BLOCK - N_META, 0))
    lane = lax.broadcasted_iota(jnp.int32, (BLOCK, LANES), 1)
    low = lane < ATT_HEAD_DIM

    kdup, vdup = [], []
    for t in range(ATT_KV_WIDTH // LANES):
        k_lo, k_hi = _lane_halves(kv_cat[:, t * LANES:(t + 1) * LANES])
        v_lo, v_hi = _lane_halves(kv_cat[:, ATT_KV_WIDTH + t * LANES:ATT_KV_WIDTH + (t + 1) * LANES])
        kdup += [k_lo.astype(BF16), k_hi.astype(BF16)]
        vdup += [v_lo.astype(BF16), v_hi.astype(BF16)]

    zero = jnp.zeros((BLOCK, LANES), BF16)
    for h in range(ATT_KV_HEADS):
        q_rows = []
        for p in range(ATT_GROUP // 2):
            q2 = qa_ref[:, (h * ATT_GROUP + 2 * p) * ATT_HEAD_DIM:(h * ATT_GROUP + 2 * p + 2) * ATT_HEAD_DIM]
            q2 = q2 * ATT_HEAD_DIM ** -0.5
            q_rows += [jnp.where(low, q2, zero), jnp.where(low, zero, q2)]
        q_all = jnp.concatenate(q_rows, axis=0)
        s_all = lax.dot_general(q_all, kdup[h], (((1,), (1,)), ((), ())), preferred_element_type=F32)
        probs, denoms = [], []
        for g in range(ATT_GROUP):
            head = h * ATT_GROUP + g
            s_head = s_all[g * BLOCK:(g + 1) * BLOCK]
            s = jnp.where(from_prev, s_head[:, :BLOCK], s_head[:, BLOCK:]) + alibi_ref[head]
            s = jnp.where(masked, MASK_VALUE, s)
            sink = sink_ref[head]
            m = jnp.maximum(jnp.max(s, axis=-1, keepdims=True), sink)
            p_ = jnp.exp(s - m)
            denoms.append(jnp.sum(p_, axis=-1, keepdims=True) + jnp.exp(sink - m))
            p_ = p_.astype(BF16)
            probs.append(jnp.concatenate([jnp.where(from_prev, p_, zero), jnp.where(from_prev, zero, p_)], axis=1))
        pv = jnp.dot(jnp.concatenate(probs, axis=0), vdup[h], preferred_element_type=F32)
        for p in range(ATT_GROUP // 2):
            even = pv[(2 * p) * BLOCK:(2 * p + 1) * BLOCK] / denoms[2 * p]
            odd = pv[(2 * p + 1) * BLOCK:(2 * p + 2) * BLOCK] / denoms[2 * p + 1]
            c0 = (h * ATT_GROUP + 2 * p) * ATT_HEAD_DIM
            o_ref[:, c0:c0 + LANES] = jnp.where(low, even, odd).astype(o_ref.dtype)

    for h in range(RET_HEADS):
        cols = slice(h * RET_DIM, (h + 1) * RET_DIM)
        q = qr_ref[:, cols]
        v = vr_ref[:, cols]
        k_scaled = kr_ref[:, cols].astype(F32) * k_scale
        s = lax.dot_general(q, k_scaled.astype(BF16), (((1,), (1,)), ((), ())), preferred_element_type=F32)
        s = s * decay_ref[h]
        inner = jnp.dot(s.astype(BF16), v, preferred_element_type=F32)
        state = state_ref[h]
        qd = (q.astype(F32) * qw_ref[h]).astype(BF16)
        cross = jnp.dot(qd, state.astype(BF16), preferred_element_type=F32)
        state_ref[h] = _retention_state_update(k_scaled, v, kw_ref[h], state, chunk_decay[h])
        ret = inner + cross
        ret = ret * lax.rsqrt(jnp.mean(ret * ret, axis=-1, keepdims=True) + NORM_EPS)
        gate = gr_ref[:, cols].astype(F32)
        o_ref[:, ATT_WIDTH + h * RET_DIM:ATT_WIDTH + (h + 1) * RET_DIM] = (
            ret * (gate * jax.nn.sigmoid(gate))).astype(o_ref.dtype)


def _mixer_core(proj, proj_meta, sinks, batch, seq):
    nchunks = seq // BLOCK
    decay, k_w, q_w, chunk_decay = _retention_constants()
    proj3 = proj.reshape(batch, seq, IN_WIDTH)
    meta_chunk = jnp.pad(proj_meta, ((BLOCK - N_META, 0), (0, 0)))
    seg = lambda s: pl.BlockSpec((batch, BLOCK, ATT_WIDTH), lambda c: (0, c, s))
    meta_seg = lambda s: pl.BlockSpec((BLOCK, ATT_WIDTH), lambda c: (0, s))
    const3 = lambda shape: pl.BlockSpec(shape, lambda c: (0, 0, 0))
    alibi = _alibi_bias()
    out = pl.pallas_call(
        functools.partial(_mixer_kernel, chunk_decay=chunk_decay),
        grid=(nchunks,),
        in_specs=[pl.BlockSpec(memory_space=pltpu.SMEM),
                  seg(SEG_QA), seg(SEG_QR), seg(SEG_KR), seg(SEG_VR), seg(SEG_GR),
                  pl.BlockSpec((batch, BLOCK, PROJ_BN), lambda c: (0, c, SEG_KVA)),
                  pl.BlockSpec((batch, BLOCK, PROJ_BN), lambda c: (0, jnp.maximum(c - 1, 0), SEG_KVA)),
                  meta_seg(META_KR), meta_seg(META_VR),
                  pl.BlockSpec((BLOCK, PROJ_BN), lambda c: (0, META_KVA)),
                  const3(alibi.shape), const3(decay.shape), const3(k_w.shape), const3(q_w.shape)],
        out_specs=pl.BlockSpec((batch, BLOCK, MIX_WIDTH), lambda c: (0, c, 0)),
        out_shape=jax.ShapeDtypeStruct((batch, seq, MIX_WIDTH), BF16),
        scratch_shapes=[pltpu.VMEM((batch, RET_HEADS, RET_DIM, RET_DIM), F32)],
        compiler_params=_params("arbitrary"),
        name="mixer_core",
    )(sinks.astype(F32), proj3, proj3, proj3, proj3, proj3, proj3, proj3,
      meta_chunk, meta_chunk, meta_chunk, alibi, decay, k_w, q_w)
    return out.reshape(batch * seq, MIX_WIDTH)


def kernel(x, meta_tokens, norm_ffn1, w_ffn1_gate_up, w_ffn1_down, norm_mix, w_in, b_in, attn_sinks, w_out,
           norm_ffn2, w_ffn2_gate_up, w_ffn2_down, norm_final):
    batch, seq, d = x.shape
    assert norm_ffn1.shape[0] == 1, "one layer: the meta rows' mixer and FFN2 outputs are never formed"
    rows = x.reshape(batch * seq, d)
    meta = meta_tokens.astype(x.dtype)

    rows_b, rows_ssq = _cast_stats(rows)
    act, wd1 = _gate_up_swiglu(rows_b, rows_ssq, norm_ffn1[0], w_ffn1_gate_up[0], to_bf16=w_ffn1_down[0])
    rows, rows_b, rows_ssq = _matmul_residual(act, wd1, rows, FFN_HALF, *DOWN_TILE, True)
    proj = _in_proj(rows_b, rows_ssq, norm_mix[0], w_in[0], b_in[0])

    meta_b, meta_ssq = _cast_stats(meta)
    act_meta = _gate_up_swiglu(meta_b, meta_ssq, norm_ffn1[0], w_ffn1_gate_up[0])
    _, meta_b, meta_ssq = _matmul_residual(act_meta, wd1, meta, FFN_HALF, *DOWN_TILE, True)
    proj_meta = _in_proj(meta_b, meta_ssq, norm_mix[0], w_in[0], b_in[0], keys_values_only=True)

    merged = _mixer_core(proj, proj_meta, attn_sinks[0], batch, seq)
    rows, rows_b, rows_ssq = _matmul_residual(merged, w_out[0], rows, 1.0, ROW_BLOCK, MXU_COLS, True)

    act, wd2 = _gate_up_swiglu(rows_b, rows_ssq, norm_ffn2[0], w_ffn2_gate_up[0], to_bf16=w_ffn2_down[0])
    rows, rows_ssq = _matmul_residual(act, wd2, rows, FFN_HALF, *DOWN_TILE, False)
    out = _scale_rows(rows, rows_ssq, norm_final)
    return out.reshape(batch, seq, d)
```

```python
import functools

import numpy as np
import jax
import jax.numpy as jnp
from jax import lax
from jax.experimental import pallas as pl
from jax.experimental.pallas import tpu as pltpu

F32 = jnp.float32
BF16 = jnp.bfloat16

N_META = 16
BLOCK = 128
ATT_HEADS = 32
ATT_KV_HEADS = 4
ATT_HEAD_DIM = 64
ATT_GROUP = ATT_HEADS // ATT_KV_HEADS
ATT_WIDTH = ATT_HEADS * ATT_HEAD_DIM
ATT_KV_WIDTH = ATT_KV_HEADS * ATT_HEAD_DIM
RET_HEADS = 8
RET_DIM = 256
RET_WIDTH = RET_HEADS * RET_DIM
MIX_WIDTH = ATT_WIDTH + RET_WIDTH
IN_WIDTH = ATT_WIDTH + 2 * ATT_KV_WIDTH + 4 * RET_WIDTH
FFN_HALF = 0.5
NORM_EPS = 1e-6
MASK_VALUE = -1e30

LANES = 128
MXU_COLS = 256
MXU_ROWS = 1024
PROJ_BN = 2 * MXU_COLS
ROW_BLOCK = 2 * MXU_ROWS
DOWN_TILE = (MXU_ROWS, MXU_COLS)
ELEMENTWISE_ROWS = 512
VMEM_LIMIT_BYTES = 63 * 1024 * 1024


def _params(*semantics):
    return pltpu.CompilerParams(dimension_semantics=semantics, vmem_limit_bytes=VMEM_LIMIT_BYTES)


def _block(total, preferred):
    if total <= preferred:
        return total
    b = preferred
    while total % b:
        b //= 2
    return b


def _lane_tile(x, width):
    return jnp.concatenate([x] * (width // LANES), axis=1)


def _lane_partial_sums(x):
    return functools.reduce(jnp.add, [x[:, t:t + LANES] for t in range(0, x.shape[1], LANES)])


def _row_scale(ssq, width, out_cols):
    total = jnp.sum(ssq, axis=-1, keepdims=True)
    rstd = lax.rsqrt(total * (1.0 / width) + NORM_EPS)
    return jnp.broadcast_to(rstd, (ssq.shape[0], out_cols))


def _row_scales_once(ssq_ref, rstd_ref, width):
    @pl.when(pl.program_id(1) == 0)
    def _():
        rstd_ref[...] = _row_scale(ssq_ref[...], width, LANES)


def _resident(shape, index_map):
    return pl.BlockSpec(shape, index_map, pipeline_mode=pl.Buffered(1))


def _gain_lanes(gain):
    return jnp.broadcast_to(gain.astype(F32)[:, None], (gain.shape[0], LANES))


def _cast_stats_kernel(x_ref, xb_ref, ssq_ref):
    x = x_ref[...]
    xb_ref[...] = x.astype(xb_ref.dtype)
    ssq_ref[...] = _lane_partial_sums(x * x)


def _cast_stats(x):
    m, d = x.shape
    bm = _block(m, ELEMENTWISE_ROWS)
    return pl.pallas_call(
        _cast_stats_kernel,
        grid=(m // bm,),
        in_specs=[pl.BlockSpec((bm, d), lambda i: (i, 0))],
        out_specs=[pl.BlockSpec((bm, d), lambda i: (i, 0)),
                   pl.BlockSpec((bm, LANES), lambda i: (i, 0))],
        out_shape=[jax.ShapeDtypeStruct((m, d), BF16), jax.ShapeDtypeStruct((m, LANES), F32)],
        compiler_params=_params("parallel"),
        name="cast_stats",
    )(x)


def _scale_rows_kernel(x_ref, ssq_ref, g_ref, o_ref):
    width = x_ref.shape[1]
    y = x_ref[...] * _row_scale(ssq_ref[...], width, width)
    o_ref[...] = (y * g_ref[...]).astype(o_ref.dtype)


def _scale_rows(x, ssq, gain):
    m, d = x.shape
    bm = _block(m, ELEMENTWISE_ROWS)
    return pl.pallas_call(
        _scale_rows_kernel,
        grid=(m // bm,),
        in_specs=[pl.BlockSpec((bm, d), lambda i: (i, 0)),
                  pl.BlockSpec((bm, LANES), lambda i: (i, 0)),
                  pl.BlockSpec((1, d), lambda i: (0, 0))],
        out_specs=pl.BlockSpec((bm, d), lambda i: (i, 0)),
        out_shape=jax.ShapeDtypeStruct((m, d), x.dtype),
        compiler_params=_params("parallel"),
        name="scale_rows",
    )(x, ssq, gain.reshape(1, d).astype(F32))


def _prepared_weight(w_ref, gain_ref):
    return (w_ref[...] * _lane_tile(gain_ref[...], w_ref.shape[1])).astype(BF16)


def _swiglu_kernel(x_ref, ssq_ref, gain_ref, wg_ref, wu_ref, *rest, with_cast):
    rstd_ref = rest[-1]
    if with_cast:
        side_ref, o_ref, side_out_ref = rest[:-1]
        side_out_ref[...] = side_ref[...].astype(side_out_ref.dtype)
    else:
        o_ref, = rest[:-1]
    _row_scales_once(ssq_ref, rstd_ref, x_ref.shape[1])
    bm, bf = o_ref.shape
    wg, wu = _prepared_weight(wg_ref, gain_ref), _prepared_weight(wu_ref, gain_ref)
    rows = min(bm, MXU_ROWS)
    for r in range(0, bm, rows):
        x = x_ref[r:r + rows, :]
        scale = _lane_tile(rstd_ref[r:r + rows, :], bf)
        g = jnp.dot(x, wg, preferred_element_type=F32) * scale
        u = jnp.dot(x, wu, preferred_element_type=F32) * scale
        o_ref[r:r + rows, :] = (g * jax.nn.sigmoid(g) * u).astype(o_ref.dtype)


def _gate_up_swiglu(xb, ssq, gain, w_gate_up, to_bf16=None):
    m, d = xb.shape
    f = w_gate_up.shape[1] // 2
    bm = _block(m, ROW_BLOCK)
    bf = _block(f, MXU_COLS)
    nf = f // bf
    in_specs = [pl.BlockSpec((bm, d), lambda i, j: (i, 0)),
                _resident((bm, LANES), lambda i, j: (i, 0)),
                _resident((d, LANES), lambda i, j: (0, 0)),
                pl.BlockSpec((d, bf), lambda i, j: (0, j)),
                pl.BlockSpec((d, bf), lambda i, j: (0, j + nf))]
    args = [xb, ssq, _gain_lanes(gain), w_gate_up, w_gate_up]
    out_specs = [pl.BlockSpec((bm, bf), lambda i, j: (i, j))]
    out_shape = [jax.ShapeDtypeStruct((m, f), BF16)]
    with_cast = to_bf16 is not None
    if with_cast:
        rows, width = to_bf16.shape
        slab = rows // ((m // bm) * nf)
        assert slab * (m // bm) * nf == rows and slab % 16 == 0, "side matrix rows must split into bf16 tiles"
        slab_spec = pl.BlockSpec((slab, width), lambda i, j: (i * nf + j, 0))
        in_specs.append(slab_spec)
        args.append(to_bf16)
        out_specs.append(slab_spec)
        out_shape.append(jax.ShapeDtypeStruct((rows, width), BF16))
    out = pl.pallas_call(
        functools.partial(_swiglu_kernel, with_cast=with_cast),
        grid=(m // bm, nf),
        in_specs=in_specs,
        out_specs=out_specs,
        out_shape=out_shape,
        scratch_shapes=[pltpu.VMEM((bm, LANES), F32)],
        compiler_params=_params("parallel", "arbitrary"),
        name="gate_up_swiglu",
    )(*args)
    return out if with_cast else out[0]


def _matmul_residual_kernel(a_ref, w_ref, r_ref, *out_refs, scale, emit_bf16):
    o_ref, ssq_ref = out_refs[0], out_refs[-1]

    @pl.when(pl.program_id(1) == 0)
    def _zero_sums():
        ssq_ref[...] = jnp.zeros_like(ssq_ref)

    w = w_ref[...].astype(BF16)
    bm = o_ref.shape[0]
    rows = min(bm, MXU_ROWS)
    for r in range(0, bm, rows):
        chunk = slice(r, r + rows)
        out = r_ref[chunk, :] + scale * jnp.dot(a_ref[chunk, :], w, preferred_element_type=F32)
        o_ref[chunk, :] = out
        if emit_bf16:
            out_refs[1][chunk, :] = out.astype(BF16)
        ssq_ref[chunk, :] += _lane_partial_sums(out * out)


def _matmul_residual(a, w, resid, scale, bm_pref, bn_pref, emit_bf16):
    m, kdim = a.shape
    n = w.shape[1]
    bm = _block(m, bm_pref)
    bn = _block(n, bn_pref)
    tile = pl.BlockSpec((bm, bn), lambda i, j: (i, j))
    out_specs = [tile] + ([tile] if emit_bf16 else []) + [pl.BlockSpec((bm, LANES), lambda i, j: (i, 0))]
    out_shape = ([jax.ShapeDtypeStruct((m, n), F32)] + ([jax.ShapeDtypeStruct((m, n), BF16)] if emit_bf16 else [])
                 + [jax.ShapeDtypeStruct((m, LANES), F32)])
    return pl.pallas_call(
        functools.partial(_matmul_residual_kernel, scale=scale, emit_bf16=emit_bf16),
        grid=(m // bm, n // bn),
        in_specs=[pl.BlockSpec((bm, kdim), lambda i, j: (i, 0)),
                  pl.BlockSpec((kdim, bn), lambda i, j: (0, j)),
                  tile],
        out_specs=out_specs,
        out_shape=out_shape,
        compiler_params=_params("parallel", "arbitrary"),
        name="matmul_residual",
    )(a, w, resid)


_QKV_BLOCKS = ATT_WIDTH // PROJ_BN
_N_PROJ_BLOCKS = IN_WIDTH // PROJ_BN
SEG_QA, SEG_QR, SEG_KR, SEG_VR, SEG_GR = 0, 1, 2, 3, 4
SEG_KVA = _N_PROJ_BLOCKS - 1


def _proj_out_block(j):
    return jnp.where(j < _QKV_BLOCKS, j, jnp.where(j == _QKV_BLOCKS, _N_PROJ_BLOCKS - 1, j - 1))


META_KR, META_VR = 0, 1
_META_PROJ_BLOCKS = 2 * RET_WIDTH // PROJ_BN + 1
META_KVA = _META_PROJ_BLOCKS - 1
META_PROJ_WIDTH = _META_PROJ_BLOCKS * PROJ_BN
_KR_FIRST_BLOCK = (ATT_WIDTH + 2 * ATT_KV_WIDTH + RET_WIDTH) // PROJ_BN


def _meta_proj_in_block(j):
    return jnp.where(j < META_KVA, j + _KR_FIRST_BLOCK, _QKV_BLOCKS)


def _in_proj_kernel(x_ref, ssq_ref, gain_ref, w_ref, b_ref, o_ref, rstd_ref):
    _row_scales_once(ssq_ref, rstd_ref, x_ref.shape[1])
    bm, bn = o_ref.shape
    w = _prepared_weight(w_ref, gain_ref)
    rows = min(bm, MXU_ROWS)
    for r in range(0, bm, rows):
        acc = jnp.dot(x_ref[r:r + rows, :], w, preferred_element_type=F32)
        acc = acc * _lane_tile(rstd_ref[r:r + rows, :], bn) + b_ref[...]
        o_ref[r:r + rows, :] = acc.astype(o_ref.dtype)


def _in_proj(xb, ssq, gain, w_in, b_in, keys_values_only=False):
    m, d = xb.shape
    bm = _block(m, ROW_BLOCK)
    if keys_values_only:
        n_blocks, width, in_block, out_block = _META_PROJ_BLOCKS, META_PROJ_WIDTH, _meta_proj_in_block, lambda j: j
    else:
        n_blocks, width, in_block, out_block = _N_PROJ_BLOCKS, IN_WIDTH, lambda j: j, _proj_out_block
    return pl.pallas_call(
        _in_proj_kernel,
        grid=(m // bm, n_blocks),
        in_specs=[pl.BlockSpec((bm, d), lambda i, j: (i, 0)),
                  _resident((bm, LANES), lambda i, j: (i, 0)),
                  _resident((d, LANES), lambda i, j: (0, 0)),
                  pl.BlockSpec((d, PROJ_BN), lambda i, j: (0, in_block(j))),
                  pl.BlockSpec((1, PROJ_BN), lambda i, j: (0, in_block(j)))],
        out_specs=pl.BlockSpec((bm, PROJ_BN), lambda i, j: (i, out_block(j))),
        out_shape=jax.ShapeDtypeStruct((m, width), BF16),
        scratch_shapes=[pltpu.VMEM((bm, LANES), F32)],
        compiler_params=_params("parallel", "arbitrary"),
        name="in_proj",
    )(xb, ssq, _gain_lanes(gain), w_in, b_in.reshape(1, IN_WIDTH).astype(F32))


def _retention_constants():
    h = np.arange(RET_HEADS, dtype=np.float64)
    log_gamma = np.log(1.0 - 2.0 ** (-5.0 - h))
    pos = np.arange(BLOCK, dtype=np.float64)
    rel = pos[:, None] - pos[None, :]
    decay = np.where(rel >= 0, np.exp(np.maximum(rel, 0.0)[None] * log_gamma[:, None, None]), 0.0)
    k_w = np.exp((BLOCK - 1 - pos)[None, :] * log_gamma[:, None])
    q_w = np.exp((pos + 1.0)[None, :] * log_gamma[:, None])
    chunk_decay = np.exp(BLOCK * log_gamma)
    wide = lambda w: np.broadcast_to(w[:, :, None], (RET_HEADS, BLOCK, RET_DIM))
    return (jnp.asarray(decay, F32), jnp.asarray(wide(k_w), F32), jnp.asarray(wide(q_w), F32),
            [float(c) for c in chunk_decay])


def _alibi_bias():
    slopes = 2.0 ** (-8.0 * np.arange(1, ATT_HEADS + 1, dtype=np.float64) / ATT_HEADS)
    pos = np.arange(BLOCK)
    dist = np.where(pos[None, :] > pos[:, None], BLOCK, 0) + pos[:, None] - pos[None, :]
    return jnp.asarray(-slopes[:, None, None] * dist[None], F32)


def _lane_halves(x):
    lane = lax.broadcasted_iota(jnp.int32, x.shape, 1)
    swapped = pltpu.roll(x, ATT_HEAD_DIM, 1)
    low = lane < ATT_HEAD_DIM
    return jnp.where(low, x, swapped), jnp.where(low, swapped, x)


def _retention_state_update(k_scaled, v, kw, state_prev, chunk_decay):
    kd = (k_scaled * kw).astype(BF16)
    kv = lax.dot_general(kd, v, (((0,), (0,)), ((), ())), preferred_element_type=F32)
    if state_prev is None:
        return kv
    return state_prev * chunk_decay + kv


def _mixer_kernel(sink_ref, qa_ref, qr_ref, kr_ref, vr_ref, gr_ref, kva_ref, kva_prev_ref,
                  mkr_ref, mvr_ref, mkva_ref, alibi_ref, decay_ref, kw_ref, qw_ref,
                  o_ref, state_ref, *, chunk_decay):
    c = pl.program_id(0)
    batch = o_ref.shape[0]

    @pl.when(c == 0)
    def _from_meta_chunk():
        for h in range(RET_HEADS):
            cols = slice(h * RET_DIM, (h + 1) * RET_DIM)
            km = mkr_ref[:, cols].astype(F32) * RET_DIM ** -0.5
            state0 = _retention_state_update(km, mvr_ref[:, cols], kw_ref[h], None, None)
            for b in range(batch):
                state_ref[b, h] = state0

    for b in range(batch):
        _mixer_chunk(c, sink_ref, qa_ref.at[b], qr_ref.at[b], kr_ref.at[b], vr_ref.at[b], gr_ref.at[b],
                     kva_ref.at[b], kva_prev_ref.at[b], mkva_ref, alibi_ref, decay_ref, kw_ref, qw_ref,
                     o_ref.at[b], state_ref.at[b], chunk_decay)


def _mixer_chunk(c, sink_ref, qa_ref, qr_ref, kr_ref, vr_ref, gr_ref, kva_ref, kva_prev_ref,
                 mkva_ref, alibi_ref, decay_ref, kw_ref, qw_ref, o_ref, state_ref, chunk_decay):
    k_scale = RET_DIM ** -0.5

    kv_prev = jnp.where(c == 0, mkva_ref[...], kva_prev_ref[...])
    kv_cat = jnp.concatenate([kv_prev, kva_ref[...]], axis=0).astype(F32)
    row = lax.broadcasted_iota(jnp.int32, (BLOCK, BLOCK), 0)
    col = lax.broadcasted_iota(jnp.int32, (BLOCK, BLOCK), 1)
    from_prev = col > row
    masked = from_prev & (col < jnp.where(c == 0, BLOCK - N_META, 0))
    lane = lax.broadcasted_iota(jnp.int32, (BLOCK, LANES), 1)
    low = lane < ATT_HEAD_DIM

    kdup, vdup = [], []
    for t in range(ATT_KV_WIDTH // LANES):
        k_lo, k_hi = _lane_halves(kv_cat[:, t * LANES:(t + 1) * LANES])
        v_lo, v_hi = _lane_halves(kv_cat[:, ATT_KV_WIDTH + t * LANES:ATT_KV_WIDTH + (t + 1) * LANES])
        kdup += [k_lo.astype(BF16), k_hi.astype(BF16)]
        vdup += [v_lo.astype(BF16), v_hi.astype(BF16)]

    zero = jnp.zeros((BLOCK, LANES), BF16)
    for h in range(ATT_KV_HEADS):
        q_rows = []
        for p in range(ATT_GROUP // 2):
            q2 = qa_ref[:, (h * ATT_GROUP + 2 * p) * ATT_HEAD_DIM:(h * ATT_GROUP + 2 * p + 2) * ATT_HEAD_DIM]
            q2 = q2 * ATT_HEAD_DIM ** -0.5
            q_rows += [jnp.where(low, q2, zero), jnp.where(low, zero, q2)]
        q_all = jnp.concatenate(q_rows, axis=0)
        s_all = lax.dot_general(q_all, kdup[h], (((1,), (1,)), ((), ())), preferred_element_type=F32)
        probs, denoms = [], []
        for g in range(ATT_GROUP):
            head = h * ATT_GROUP + g
            s_head = s_all[g * BLOCK:(g + 1) * BLOCK]
            s = jnp.where(from_prev, s_head[:, :BLOCK], s_head[:, BLOCK:]) + alibi_ref[head]
            s = jnp.where(masked, MASK_VALUE, s)
            sink = sink_ref[head]
            m = jnp.maximum(jnp.max(s, axis=-1, keepdims=True), sink)
            p_ = jnp.exp(s - m)
            denoms.append(jnp.sum(p_, axis=-1, keepdims=True) + jnp.exp(sink - m))
            p_ = p_.astype(BF16)
            probs.append(jnp.concatenate([jnp.where(from_prev, p_, zero), jnp.where(from_prev, zero, p_)], axis=1))
        pv = jnp.dot(jnp.concatenate(probs, axis=0), vdup[h], preferred_element_type=F32)
        for p in range(ATT_GROUP // 2):
            even = pv[(2 * p) * BLOCK:(2 * p + 1) * BLOCK] / denoms[2 * p]
            odd = pv[(2 * p + 1) * BLOCK:(2 * p + 2) * BLOCK] / denoms[2 * p + 1]
            c0 = (h * ATT_GROUP + 2 * p) * ATT_HEAD_DIM
            o_ref[:, c0:c0 + LANES] = jnp.where(low, even, odd).astype(o_ref.dtype)

    for h in range(RET_HEADS):
        cols = slice(h * RET_DIM, (h + 1) * RET_DIM)
        q = qr_ref[:, cols]
        v = vr_ref[:, cols]
        k_scaled = kr_ref[:, cols].astype(F32) * k_scale
        s = lax.dot_general(q, k_scaled.astype(BF16), (((1,), (1,)), ((), ())), preferred_element_type=F32)
        s = s * decay_ref[h]
        inner = jnp.dot(s.astype(BF16), v, preferred_element_type=F32)
        state = state_ref[h]
        qd = (q.astype(F32) * qw_ref[h]).astype(BF16)
        cross = jnp.dot(qd, state.astype(BF16), preferred_element_type=F32)
        state_ref[h] = _retention_state_update(k_scaled, v, kw_ref[h], state, chunk_decay[h])
        ret = inner + cross
        ret = ret * lax.rsqrt(jnp.mean(ret * ret, axis=-1, keepdims=True) + NORM_EPS)
        gate = gr_ref[:, cols].astype(F32)
        o_ref[:, ATT_WIDTH + h * RET_DIM:ATT_WIDTH + (h + 1) * RET_DIM] = (
            ret * (gate * jax.nn.sigmoid(gate))).astype(o_ref.dtype)


def _mixer_core(proj, proj_meta, sinks, batch, seq):
    nchunks = seq // BLOCK
    decay, k_w, q_w, chunk_decay = _retention_constants()
    proj3 = proj.reshape(batch, seq, IN_WIDTH)
    meta_chunk = jnp.pad(proj_meta, ((BLOCK - N_META, 0), (0, 0)))
    seg = lambda s: pl.BlockSpec((batch, BLOCK, ATT_WIDTH), lambda c: (0, c, s))
    meta_seg = lambda s: pl.BlockSpec((BLOCK, ATT_WIDTH), lambda c: (0, s))
    const3 = lambda shape: pl.BlockSpec(shape, lambda c: (0, 0, 0))
    alibi = _alibi_bias()
    out = pl.pallas_call(
        functools.partial(_mixer_kernel, chunk_decay=chunk_decay),
        grid=(nchunks,),
        in_specs=[pl.BlockSpec(memory_space=pltpu.SMEM),
                  seg(SEG_QA), seg(SEG_QR), seg(SEG_KR), seg(SEG_VR), seg(SEG_GR),
                  pl.BlockSpec((batch, BLOCK, PROJ_BN), lambda c: (0, c, SEG_KVA)),
                  pl.BlockSpec((batch, BLOCK, PROJ_BN), lambda c: (0, jnp.maximum(c - 1, 0), SEG_KVA)),
                  meta_seg(META_KR), meta_seg(META_VR),
                  pl.BlockSpec((BLOCK, PROJ_BN), lambda c: (0, META_KVA)),
                  const3(alibi.shape), const3(decay.shape), const3(k_w.shape), const3(q_w.shape)],
        out_specs=pl.BlockSpec((batch, BLOCK, MIX_WIDTH), lambda c: (0, c, 0)),
        out_shape=jax.ShapeDtypeStruct((batch, seq, MIX_WIDTH), BF16),
        scratch_shapes=[pltpu.VMEM((batch, RET_HEADS, RET_DIM, RET_DIM), F32)],
        compiler_params=_params("arbitrary"),
        name="mixer_core",
    )(sinks.astype(F32), proj3, proj3, proj3, proj3, proj3, proj3, proj3,
      meta_chunk, meta_chunk, meta_chunk, alibi, decay, k_w, q_w)
    return out.reshape(batch * seq, MIX_WIDTH)


def kernel(x, meta_tokens, norm_ffn1, w_ffn1_gate_up, w_ffn1_down, norm_mix, w_in, b_in, attn_sinks, w_out,
           norm_ffn2, w_ffn2_gate_up, w_ffn2_down, norm_final):
    batch, seq, d = x.shape
    assert norm_ffn1.shape[0] == 1, "one layer: the meta rows' mixer and FFN2 outputs are never formed"
    rows = x.reshape(batch * seq, d)
    meta = meta_tokens.astype(x.dtype)

    rows_b, rows_ssq = _cast_stats(rows)
    act, wd1 = _gate_up_swiglu(rows_b, rows_ssq, norm_ffn1[0], w_ffn1_gate_up[0], to_bf16=w_ffn1_down[0])
    rows, rows_b, rows_ssq = _matmul_residual(act, wd1, rows, FFN_HALF, *DOWN_TILE, True)
    proj = _in_proj(rows_b, rows_ssq, norm_mix[0], w_in[0], b_in[0])

    meta_b, meta_ssq = _cast_stats(meta)
    act_meta = _gate_up_swiglu(meta_b, meta_ssq, norm_ffn1[0], w_ffn1_gate_up[0])
    _, meta_b, meta_ssq = _matmul_residual(act_meta, wd1, meta, FFN_HALF, *DOWN_TILE, True)
    proj_meta = _in_proj(meta_b, meta_ssq, norm_mix[0], w_in[0], b_in[0], keys_values_only=True)

    merged = _mixer_core(proj, proj_meta, attn_sinks[0], batch, seq)
    rows, rows_b, rows_ssq = _matmul_residual(merged, w_out[0], rows, 1.0, ROW_BLOCK, MXU_COLS, True)

    act, wd2 = _gate_up_swiglu(rows_b, rows_ssq, norm_ffn2[0], w_ffn2_gate_up[0], to_bf16=w_ffn2_down[0])
    rows, rows_ssq = _matmul_residual(act, wd2, rows, FFN_HALF, *DOWN_TILE, False)
    out = _scale_rows(rows, rows_ssq, norm_final)
    return out.reshape(batch, seq, d)
```
